```python
import math
import jax, jax.numpy as jnp
from jax import lax
import numpy as np

D_MODEL = 1024
BATCH = 8
SEQ = 8192
DEPTH = 2

D_MIX = D_MODEL
A_HEADS = 8
A_NOPE = 64
A_ROPE = 32
A_V = 64
A_Q_RANK = 384
A_KV_RANK = 256
B_HEADS = 8
B_KV_HEADS = 2
B_HEAD_DIM = 64
B_GROUP = B_HEADS // B_KV_HEADS
WINDOW = 128
BLOCK = 128
REL_BUCKETS = 32
REL_MAX_DIST = 128
D_FF = 2816
FFN_RES_WEIGHT = 0.5
ROPE_THETA = 10000.0
EPS = 1e-6
Q_BLOCK = 128
NEG_INF = -1e30

IN_COLS = A_Q_RANK + A_KV_RANK + A_ROPE + B_HEADS * B_HEAD_DIM + 2 * B_KV_HEADS * B_HEAD_DIM

kernel_name = "hybrid_mla_swa_macaron_encoder"


def rmsnorm(x, g):
    xf = x.astype(jnp.float32)
    y = xf * lax.rsqrt(jnp.mean(xf * xf, axis=-1, keepdims=True) + EPS)
    return (y * g.astype(jnp.float32)).astype(x.dtype)


def swiglu(x, w_gate, w_up, w_down):
    return (jax.nn.silu(x @ w_gate) * (x @ w_up)) @ w_down


def rope_tables(seq):
    pos = jnp.arange(seq, dtype=jnp.float32)
    inv = ROPE_THETA ** (-jnp.arange(0, A_ROPE, 2, dtype=jnp.float32) / A_ROPE)
    ang = pos[:, None] * inv[None, :]
    return jnp.cos(ang), jnp.sin(ang)


def apply_rope(x, cos, sin):
    x1, x2 = jnp.split(x, 2, axis=-1)
    out = jnp.concatenate([x1 * cos - x2 * sin, x2 * cos + x1 * sin], axis=-1)
    return out.astype(x.dtype)


def t5_bucket(rel):
    nb = REL_BUCKETS // 2
    max_exact = nb // 2
    bucket = jnp.where(rel > 0, nb, 0)
    n = jnp.abs(rel)
    nf = jnp.maximum(n, 1).astype(jnp.float32)
    large = max_exact + (jnp.log(nf / max_exact) / math.log(REL_MAX_DIST / max_exact)
                         * (nb - max_exact)).astype(jnp.int32)
    large = jnp.minimum(large, nb - 1)
    return bucket + jnp.where(n < max_exact, n, large)


def band_bias(rel_bias):
    r = jnp.arange(BLOCK)[:, None]
    j = jnp.arange(3 * BLOCK)[None, :]
    rel = j - BLOCK - r
    bias = rel_bias[t5_bucket(rel)]
    bias = jnp.transpose(bias, (2, 0, 1)).reshape(B_KV_HEADS, B_GROUP, BLOCK, 3 * BLOCK)
    in_win = jnp.abs(rel) <= WINDOW
    return bias.astype(jnp.float32), in_win


def mla_attention(c_q, c_kv, k_rope, q_norm_g, w_uq, kv_norm_g, w_ukv, cos, sin):
    B, S, _ = c_q.shape
    q = (rmsnorm(c_q, q_norm_g) @ w_uq).reshape(B, S, A_HEADS, A_NOPE + A_ROPE)
    q_nope, q_rope = q[..., :A_NOPE], q[..., A_NOPE:]
    q_rope = apply_rope(q_rope, cos[None, :, None, :], sin[None, :, None, :])
    kv = (rmsnorm(c_kv, kv_norm_g) @ w_ukv).reshape(B, S, A_HEADS, A_NOPE + A_V)
    k_nope, v = kv[..., :A_NOPE], kv[..., A_NOPE:]
    k_rope = apply_rope(k_rope, cos[None], sin[None])
    scale = (A_NOPE + A_ROPE) ** -0.5
    nblk = S // Q_BLOCK
    qn_b = q_nope.reshape(B, nblk, Q_BLOCK, A_HEADS, A_NOPE).transpose(1, 0, 2, 3, 4)
    qr_b = q_rope.reshape(B, nblk, Q_BLOCK, A_HEADS, A_ROPE).transpose(1, 0, 2, 3, 4)

    def attend(blk):
        qn, qr = blk
        s = (jnp.einsum('bqhd,bkhd->bhqk', qn, k_nope)
             + jnp.einsum('bqhr,bkr->bhqk', qr, k_rope))
        p = jax.nn.softmax(s.astype(jnp.float32) * scale, axis=-1).astype(v.dtype)
        return jnp.einsum('bhqk,bkhd->bqhd', p, v)

    out = lax.map(attend, (qn_b, qr_b))
    return out.transpose(1, 0, 2, 3, 4).reshape(B, S, A_HEADS * A_V)


def window_gqa(q, k, v, sink, bias, in_win):
    B, S, _ = q.shape
    nblk = S // BLOCK
    qb = q.reshape(B, nblk, BLOCK, B_KV_HEADS, B_GROUP, B_HEAD_DIM)

    def banded(t):
        t = t.reshape(B, S, B_KV_HEADS, B_HEAD_DIM)
        tp = jnp.pad(t, ((0, 0), (BLOCK, BLOCK), (0, 0), (0, 0)))
        tp = tp.reshape(B, nblk + 2, BLOCK, B_KV_HEADS, B_HEAD_DIM)
        return jnp.concatenate([tp[:, :-2], tp[:, 1:-1], tp[:, 2:]], axis=2)

    kw, vw = banded(k), banded(v)
    scale = B_HEAD_DIM ** -0.5
    s = jnp.einsum('bnqkgd,bnjkd->bnkgqj', qb, kw).astype(jnp.float32) * scale
    s = s + bias[None, None]
    key_pos = (jnp.arange(nblk)[:, None] - 1) * BLOCK + jnp.arange(3 * BLOCK)[None, :]
    valid = (key_pos >= 0) & (key_pos < S)
    mask = in_win[None, :, :] & valid[:, None, :]
    s = jnp.where(mask[None, :, None, None], s, NEG_INF)
    sk = sink.astype(jnp.float32).reshape(B_KV_HEADS, B_GROUP)[None, None, :, :, None, None]
    m = jnp.maximum(jnp.max(s, axis=-1, keepdims=True), sk)
    p = jnp.exp(s - m)
    p = p / (jnp.sum(p, axis=-1, keepdims=True) + jnp.exp(sk - m))
    out = jnp.einsum('bnkgqj,bnjkd->bnqkgd', p.astype(vw.dtype), vw)
    return out.reshape(B, S, B_HEADS * B_HEAD_DIM)


def setup_inputs(seed: int = 0) -> dict:
    key = jax.random.key(seed)
    ks = iter(jax.random.split(key, 32))

    def w(shape, fan_in):
        return jax.random.normal(next(ks), shape, jnp.float32) * fan_in ** -0.5

    def g(shape):
        return 1.0 + 0.05 * jax.random.normal(next(ks), shape, jnp.float32)

    L = DEPTH
    return {
        "x": jax.random.normal(next(ks), (BATCH, SEQ, D_MODEL), jnp.float32),
        "rel_bias": 0.1 * jax.random.normal(next(ks), (REL_BUCKETS, B_HEADS), jnp.float32),
        "ffn1_pre_g": g((L, D_MODEL)),
        "ffn1_w_gate": w((L, D_MODEL, D_FF), D_MODEL),
        "ffn1_w_up": w((L, D_MODEL, D_FF), D_MODEL),
        "ffn1_w_down": w((L, D_FF, D_MODEL), D_FF),
        "ffn1_post_g": g((L, D_MODEL)),
        "mix_pre_g": g((L, D_MODEL)),
        "w_in": w((L, D_MODEL, IN_COLS), D_MODEL),
        "mla_q_norm_g": g((L, A_Q_RANK)),
        "mla_w_uq": w((L, A_Q_RANK, A_HEADS * (A_NOPE + A_ROPE)), A_Q_RANK),
        "mla_kv_norm_g": g((L, A_KV_RANK)),
        "mla_w_ukv": w((L, A_KV_RANK, A_HEADS * (A_NOPE + A_V)), A_KV_RANK),
        "swa_sink": 0.5 * jax.random.normal(next(ks), (L, B_HEADS), jnp.float32),
        "w_out": w((L, D_MIX, D_MODEL), D_MIX),
        "mix_post_g": g((L, D_MODEL)),
        "ffn2_pre_g": g((L, D_MODEL)),
        "ffn2_w_gate": w((L, D_MODEL, D_FF), D_MODEL),
        "ffn2_w_up": w((L, D_MODEL, D_FF), D_MODEL),
        "ffn2_w_down": w((L, D_FF, D_MODEL), D_FF),
        "ffn2_post_g": g((L, D_MODEL)),
    }


def reference(x, rel_bias, ffn1_pre_g, ffn1_w_gate, ffn1_w_up, ffn1_w_down, ffn1_post_g,
              mix_pre_g, w_in, mla_q_norm_g, mla_w_uq, mla_kv_norm_g, mla_w_ukv, swa_sink,
              w_out, mix_post_g, ffn2_pre_g, ffn2_w_gate, ffn2_w_up, ffn2_w_down, ffn2_post_g):
    S = x.shape[1]
    cos, sin = rope_tables(S)
    bias, in_win = band_bias(rel_bias)
    splits = list(np.cumsum([A_Q_RANK, A_KV_RANK, A_ROPE,
                             B_HEADS * B_HEAD_DIM, B_KV_HEADS * B_HEAD_DIM]))
    for i in range(DEPTH):
        h = rmsnorm(x, ffn1_pre_g[i])
        x = x + FFN_RES_WEIGHT * rmsnorm(swiglu(h, ffn1_w_gate[i], ffn1_w_up[i], ffn1_w_down[i]),
                                         ffn1_post_g[i])
        h = rmsnorm(x, mix_pre_g[i])
        z = h @ w_in[i]
        c_q, c_kv, k_rope, q_b, k_b, v_b = jnp.split(z, splits, axis=-1)
        o_a = mla_attention(c_q, c_kv, k_rope, mla_q_norm_g[i], mla_w_uq[i],
                            mla_kv_norm_g[i], mla_w_ukv[i], cos, sin)
        o_b = window_gqa(q_b, k_b, v_b, swa_sink[i], bias, in_win)
        o = jnp.concatenate([o_a, o_b], axis=-1) @ w_out[i]
        x = x + rmsnorm(o, mix_post_g[i])
        h = rmsnorm(x, ffn2_pre_g[i])
        x = x + FFN_RES_WEIGHT * rmsnorm(swiglu(h, ffn2_w_gate[i], ffn2_w_up[i], ffn2_w_down[i]),
                                         ffn2_post_g[i])
    return x
```

```python
import functools
import math

import numpy as np
import jax
import jax.numpy as jnp
from jax import lax
from jax.experimental import pallas as pl
from jax.experimental.pallas import tpu as pltpu

D_MODEL = 1024
A_HEADS = 8
A_NOPE = 64
A_ROPE = 32
A_V = 64
A_Q_RANK = 384
A_KV_RANK = 256
B_HEADS = 8
B_KV_HEADS = 2
B_HEAD_DIM = 64
B_GROUP = B_HEADS // B_KV_HEADS
WINDOW = 128
BLOCK = 128
REL_BUCKETS = 32
REL_MAX_DIST = 128
D_FF = 2816
FFN_RES_WEIGHT = 0.5
ROPE_THETA = 10000.0
EPS = 1e-6
NEG_INF = -1e30

LANES = 128
V_ROWS = 80
SWA_COLS = B_HEADS * B_HEAD_DIM
VMEM_LIMIT = 56 * 1024 * 1024

FFN_TM = 512
FFN_FC = 256
PROJ_TM = 512
FLASH_TQ = 1024
FLASH_TK = PROJ_TM
SWA_TQ = 1024
OUT_TM = 1024

_C_Q = 0
_C_KV = _C_Q + A_Q_RANK
_C_KR = _C_KV + A_KV_RANK
_C_KRR = _C_KR + LANES
_C_QS = _C_KRR + LANES
_C_KS = _C_QS + SWA_COLS
_C_VS = _C_KS + 4 * LANES
N_IN = _C_VS + 4 * LANES

BF16 = jnp.bfloat16
F32 = jnp.float32


def _dot(a, b):
    return jnp.dot(a, b, preferred_element_type=F32)


def _dot_nt(a, b):
    return lax.dot_general(a, b, (((1,), (1,)), ((), ())), preferred_element_type=F32)


def _rms(x, g):
    return x * lax.rsqrt(jnp.mean(x * x, axis=-1, keepdims=True) + EPS) * g


def _const_spec(shape):
    nd = len(shape)
    return pl.BlockSpec(shape, lambda *_: (0,) * nd, pipeline_mode=pl.Buffered(1))


def _ffn_kernel(x_ref, pre_ref, wg_ref, wu_ref, wd_ref, post_ref, o_ref):
    x = x_ref[...]
    h = _rms(x, pre_ref[...]).astype(BF16)
    acc = None
    for c in range(0, D_FF, FFN_FC):
        g = _dot(h, wg_ref[:, c:c + FFN_FC])
        u = _dot(h, wu_ref[:, c:c + FFN_FC])
        a = (g * u / (1.0 + jnp.exp(-g))).astype(BF16)
        d = _dot(a, wd_ref[c:c + FFN_FC, :])
        acc = d if acc is None else acc + d
    o_ref[...] = x + FFN_RES_WEIGHT * _rms(acc, post_ref[...])


def _ffn(x2, pre_g, wg, wu, wd, post_g):
    m = x2.shape[0]
    row = pl.BlockSpec((FFN_TM, D_MODEL), lambda i: (i, 0))
    return pl.pallas_call(
        _ffn_kernel,
        grid=(m // FFN_TM,),
        in_specs=[row, _const_spec((1, D_MODEL)), _const_spec((D_MODEL, D_FF)),
                  _const_spec((D_MODEL, D_FF)), _const_spec((D_FF, D_MODEL)),
                  _const_spec((1, D_MODEL))],
        out_specs=row,
        out_shape=jax.ShapeDtypeStruct((m, D_MODEL), F32),
        compiler_params=pltpu.CompilerParams(
            dimension_semantics=("parallel",), vmem_limit_bytes=VMEM_LIMIT),
        name="ffn",
    )(x2, pre_g, wg, wu, wd, post_g)


def _proj_kernel(x_ref, pre_ref, win_ref, qn_ref, wqt_ref, kvn_ref, wk_ref, wvt_ref,
                 cq_ref, sq_ref, ck_ref, sk_ref,
                 qt_ref, k_ref, vt_ref, qs_ref, ks_ref, vs_ref):
    h = _rms(x_ref[0], pre_ref[...]).astype(BF16)
    z = _dot(h, win_ref[...])
    cqn = _rms(z[:, _C_Q:_C_KV], qn_ref[...]).astype(BF16)
    ckvn = _rms(z[:, _C_KV:_C_KR], kvn_ref[...]).astype(BF16)

    qt2 = _dot_nt(wqt_ref[...], cqn)
    half = A_HEADS * LANES
    for hd in range(A_HEADS):
        lo = hd * LANES
        qt_ref[0, hd] = (qt2[lo:lo + LANES] * cq_ref[...]
                         + qt2[half + lo:half + lo + LANES] * sq_ref[...]).astype(BF16)

    kr = z[:, _C_KR:_C_KRR] * ck_ref[...] + z[:, _C_KRR:_C_QS] * sk_ref[...]
    kn = _dot(ckvn, wk_ref[...])
    for hd in range(A_HEADS):
        lo = hd * LANES
        k_ref[0, hd] = (kn[:, lo:lo + LANES] + kr).astype(BF16)

    vt = _dot_nt(wvt_ref[...], ckvn)
    ones_row = (lax.broadcasted_iota(jnp.int32, (V_ROWS, 1), 0) == A_V).astype(F32)
    for hd in range(A_HEADS):
        lo = hd * V_ROWS
        vt_ref[0, hd, 0] = (vt[lo:lo + V_ROWS] + ones_row).astype(BF16)

    qs_ref[0] = (z[:, _C_QS:_C_KS] * (B_HEAD_DIM ** -0.5)).astype(BF16)
    ks_ref[0] = z[:, _C_KS:_C_VS].astype(BF16)
    vs_ref[0] = z[:, _C_VS:N_IN].astype(BF16)


def _proj(x3, pre_g, win, qn_g, wqt, kvn_g, wk, wvt, cq, sq, ck, sk):
    b, s, _ = x3.shape
    tm = PROJ_TM
    nt = s // tm
    grid = (b, nt)
    in_specs = [
        pl.BlockSpec((1, tm, D_MODEL), lambda i, j: (i, j, 0)),
        _const_spec((1, D_MODEL)), _const_spec((D_MODEL, N_IN)),
        _const_spec((1, A_Q_RANK)), _const_spec((2 * A_HEADS * LANES, A_Q_RANK)),
        _const_spec((1, A_KV_RANK)), _const_spec((A_KV_RANK, A_HEADS * LANES)),
        _const_spec((A_HEADS * V_ROWS, A_KV_RANK)),
        pl.BlockSpec((LANES, tm), lambda i, j: (0, j)),
        pl.BlockSpec((LANES, tm), lambda i, j: (0, j)),
        pl.BlockSpec((tm, LANES), lambda i, j: (j, 0)),
        pl.BlockSpec((tm, LANES), lambda i, j: (j, 0)),
    ]
    out_specs = [
        pl.BlockSpec((1, A_HEADS, LANES, tm), lambda i, j: (i, 0, 0, j)),
        pl.BlockSpec((1, A_HEADS, tm, LANES), lambda i, j: (i, 0, j, 0)),
        pl.BlockSpec((1, A_HEADS, 1, V_ROWS, tm), lambda i, j: (i, 0, j, 0, 0)),
        pl.BlockSpec((1, tm, SWA_COLS), lambda i, j: (i, j, 0)),
        pl.BlockSpec((1, tm, 4 * LANES), lambda i, j: (i, j, 0)),
        pl.BlockSpec((1, tm, 4 * LANES), lambda i, j: (i, j, 0)),
    ]
    out_shape = [
        jax.ShapeDtypeStruct((b, A_HEADS, LANES, s), BF16),
        jax.ShapeDtypeStruct((b, A_HEADS, s, LANES), BF16),
        jax.ShapeDtypeStruct((b, A_HEADS, nt, V_ROWS, tm), BF16),
        jax.ShapeDtypeStruct((b, s, SWA_COLS), BF16),
        jax.ShapeDtypeStruct((b, s, 4 * LANES), BF16),
        jax.ShapeDtypeStruct((b, s, 4 * LANES), BF16),
    ]
    return pl.pallas_call(
        _proj_kernel, grid=grid, in_specs=in_specs, out_specs=out_specs, out_shape=out_shape,
        compiler_params=pltpu.CompilerParams(
            dimension_semantics=("parallel", "parallel"), vmem_limit_bytes=VMEM_LIMIT),
        name="mix_proj",
    )(x3, pre_g, win, qn_g, wqt, kvn_g, wk, wvt, cq, sq, ck, sk)


def _flash_kernel(qt_ref, k_ref, vt_ref, o_ref, m_ref, acc_ref, *, nk):
    tk = FLASH_TK
    m_ref[...] = jnp.full(m_ref.shape, NEG_INF, F32)
    acc_ref[...] = jnp.zeros(acc_ref.shape, F32)

    def body(i, carry):
        off = pl.multiple_of(i * tk, tk)
        for j in range(2):
            kc = k_ref[0, j, pl.ds(off, tk), :]
            st = _dot(kc, qt_ref[0, j])
            m_prev = m_ref[j]
            m_new = jnp.maximum(m_prev, jnp.max(st, axis=0, keepdims=True))
            alpha = jnp.exp(m_prev - m_new)
            p = jnp.exp(st - m_new).astype(BF16)
            acc_ref[j] = alpha * acc_ref[j] + _dot(vt_ref[0, j, i], p)
            m_ref[j] = m_new
        return carry

    lax.fori_loop(0, nk, body, 0)
    outs = []
    for j in range(2):
        acc = acc_ref[j]
        outs.append(acc[:A_V] / acc[A_V:A_V + 1])
    o_ref[0] = jnp.concatenate(outs, axis=0).T.astype(BF16)


def _flash(qt, k, vt):
    b, _, _, s = qt.shape
    tq = FLASH_TQ
    nk = s // FLASH_TK
    grid = (b, A_HEADS // 2, s // tq)
    return pl.pallas_call(
        functools.partial(_flash_kernel, nk=nk),
        grid=grid,
        in_specs=[
            pl.BlockSpec((1, 2, LANES, tq), lambda i, h, j: (i, h, 0, j)),
            pl.BlockSpec((1, 2, s, LANES), lambda i, h, j: (i, h, 0, 0)),
            pl.BlockSpec((1, 2, nk, V_ROWS, FLASH_TK), lambda i, h, j: (i, h, 0, 0, 0)),
        ],
        out_specs=pl.BlockSpec((1, tq, LANES), lambda i, h, j: (i, j, h)),
        out_shape=jax.ShapeDtypeStruct((b, s, A_HEADS * A_V), BF16),
        scratch_shapes=[pltpu.VMEM((2, 1, tq), F32), pltpu.VMEM((2, V_ROWS, tq), F32)],
        compiler_params=pltpu.CompilerParams(
            dimension_semantics=("parallel", "parallel", "arbitrary"),
            vmem_limit_bytes=VMEM_LIMIT),
        name="mla_flash",
    )(qt, k, vt)


def _bias_kernel(tab_ref, bkt_ref, o_ref):
    for v in range(3):
        bkt = bkt_ref[v]
        for hd in range(B_HEADS):
            acc = jnp.full(bkt.shape, NEG_INF, F32)
            for i in range(REL_BUCKETS):
                acc = jnp.where(bkt == i, tab_ref[i, hd], acc)
            o_ref[v, hd] = acc


def _band_bias(rel_bias, buckets):
    return pl.pallas_call(
        _bias_kernel,
        in_specs=[pl.BlockSpec(memory_space=pltpu.SMEM),
                  pl.BlockSpec(memory_space=pltpu.VMEM)],
        out_specs=pl.BlockSpec(memory_space=pltpu.VMEM),
        out_shape=jax.ShapeDtypeStruct((3, B_HEADS, BLOCK, 3 * BLOCK), F32),
        name="band_bias",
    )(rel_bias, buckets)


def _swa_kernel(sink_ref, q_ref, k_ref, v_ref, bias_ref, o_ref, *, nblk, seq):
    step = pl.program_id(1)
    per_step = SWA_TQ // BLOCK
    win = 3 * BLOCK

    def blk(t, carry):
        n = step * per_step + t
        ks = pl.multiple_of(jnp.clip((n - 1) * BLOCK, 0, seq - win), BLOCK)
        var = jnp.where(n == 0, 1, jnp.where(n == nblk - 1, 2, 0))
        r0 = pl.multiple_of(t * BLOCK, BLOCK)
        for pair in range(B_HEADS // 2):
            kvh = (2 * pair) // B_GROUP
            qp = q_ref[0, pl.ds(r0, BLOCK), pair * LANES:(pair + 1) * LANES]
            acc = None
            for e in range(2):
                hd = 2 * pair + e
                c0 = (2 * kvh + e) * LANES
                kk = k_ref[0, pl.ds(ks, win), c0:c0 + LANES]
                vv = v_ref[0, pl.ds(ks, win), c0:c0 + LANES]
                s = _dot_nt(qp, kk) + bias_ref[var, hd]
                sk = sink_ref[hd]
                m = jnp.maximum(jnp.max(s, axis=1, keepdims=True), sk)
                p = jnp.exp(s - m)
                den = jnp.sum(p, axis=1, keepdims=True) + jnp.exp(sk - m)
                o = _dot(p.astype(BF16), vv) / den
                acc = o if acc is None else acc + o
            o_ref[0, pl.ds(r0, BLOCK), pair * LANES:(pair + 1) * LANES] = acc.astype(BF16)
        return carry

    lax.fori_loop(0, per_step, blk, 0)


def _swa(sink, qs, ks4, vs4, biasm):
    b, s, _ = qs.shape
    nblk = s // BLOCK
    grid = (b, s // SWA_TQ)
    return pl.pallas_call(
        functools.partial(_swa_kernel, nblk=nblk, seq=s),
        grid=grid,
        in_specs=[
            pl.BlockSpec(memory_space=pltpu.SMEM),
            pl.BlockSpec((1, SWA_TQ, SWA_COLS), lambda i, j: (i, j, 0)),
            pl.BlockSpec((1, s, 4 * LANES), lambda i, j: (i, 0, 0)),
            pl.BlockSpec((1, s, 4 * LANES), lambda i, j: (i, 0, 0)),
            _const_spec((3, B_HEADS, BLOCK, 3 * BLOCK)),
        ],
        out_specs=pl.BlockSpec((1, SWA_TQ, SWA_COLS), lambda i, j: (i, j, 0)),
        out_shape=jax.ShapeDtypeStruct((b, s, SWA_COLS), BF16),
        compiler_params=pltpu.CompilerParams(
            dimension_semantics=("parallel", "arbitrary"), vmem_limit_bytes=VMEM_LIMIT),
        name="swa",
    )(sink, qs, ks4, vs4, biasm)


def _outproj_kernel(oa_ref, ob_ref, x_ref, wo_ref, g_ref, o_ref):
    half = A_HEADS * A_V
    o = _dot(oa_ref[...], wo_ref[:half, :]) + _dot(ob_ref[...], wo_ref[half:, :])
    o_ref[...] = x_ref[...] + _rms(o, g_ref[...])


def _outproj(oa, ob, x2, wo, g):
    m = x2.shape[0]
    tm = OUT_TM
    half = A_HEADS * A_V
    return pl.pallas_call(
        _outproj_kernel,
        grid=(m // tm,),
        in_specs=[pl.BlockSpec((tm, half), lambda i: (i, 0)),
                  pl.BlockSpec((tm, SWA_COLS), lambda i: (i, 0)),
                  pl.BlockSpec((tm, D_MODEL), lambda i: (i, 0)),
                  _const_spec((D_MODEL, D_MODEL)), _const_spec((1, D_MODEL))],
        out_specs=pl.BlockSpec((tm, D_MODEL), lambda i: (i, 0)),
        out_shape=jax.ShapeDtypeStruct((m, D_MODEL), F32),
        compiler_params=pltpu.CompilerParams(
            dimension_semantics=("parallel",), vmem_limit_bytes=VMEM_LIMIT),
        name="out_proj",
    )(oa, ob, x2, wo, g)


def _t5_bucket(rel):
    nb = REL_BUCKETS // 2
    max_exact = nb // 2
    bucket = jnp.where(rel > 0, nb, 0)
    n = jnp.abs(rel)
    nf = jnp.maximum(n, 1).astype(jnp.float32)
    large = max_exact + (jnp.log(nf / max_exact) / math.log(REL_MAX_DIST / max_exact)
                         * (nb - max_exact)).astype(jnp.int32)
    large = jnp.minimum(large, nb - 1)
    return bucket + jnp.where(n < max_exact, n, large)


def _band_buckets():
    r = jnp.arange(BLOCK)[None, :, None]
    jj = jnp.arange(3 * BLOCK)[None, None, :]
    shift = jnp.array([0, BLOCK, -BLOCK])[:, None, None]
    j = jj + shift
    rel = j - BLOCK - r
    ok = (j >= 0) & (j < 3 * BLOCK) & (jnp.abs(rel) <= WINDOW)
    return jnp.where(ok, _t5_bucket(rel), -1).astype(jnp.int32)


def _rope_tables(seq):
    pos = jnp.arange(seq, dtype=jnp.float32)
    inv = ROPE_THETA ** (-jnp.arange(0, A_ROPE, 2, dtype=jnp.float32) / A_ROPE)
    ang = pos[:, None] * inv[None, :]
    cos, sin = jnp.cos(ang), jnp.sin(ang)
    cos2 = jnp.concatenate([cos, cos], axis=1)
    sin2 = jnp.concatenate([sin, sin], axis=1)
    z32 = jnp.zeros((seq, LANES - A_NOPE - A_ROPE), F32)
    ck = jnp.concatenate([jnp.zeros((seq, A_NOPE), F32), cos2, z32], axis=1)
    sk = jnp.concatenate([jnp.zeros((seq, A_NOPE), F32), sin2, z32], axis=1)
    scale = (A_NOPE + A_ROPE) ** -0.5
    cq = jnp.concatenate([jnp.ones((seq, A_NOPE), F32), cos2, z32], axis=1) * scale
    return cq.T, (sk * scale).T, ck, sk


def _layout_tables():
    hr = A_ROPE // 2
    rot_src = np.concatenate([np.arange(hr, A_ROPE), np.arange(0, hr)])
    rot_sgn = np.concatenate([-np.ones(hr), np.ones(hr)])

    src = np.zeros(N_IN, np.int32)
    sgn = np.zeros(N_IN, np.float32)
    src[_C_Q:_C_KR] = np.arange(A_Q_RANK + A_KV_RANK)
    sgn[_C_Q:_C_KR] = 1.0
    kr0 = A_Q_RANK + A_KV_RANK
    src[_C_KR + A_NOPE:_C_KR + A_NOPE + A_ROPE] = kr0 + np.arange(A_ROPE)
    sgn[_C_KR + A_NOPE:_C_KR + A_NOPE + A_ROPE] = 1.0
    src[_C_KRR + A_NOPE:_C_KRR + A_NOPE + A_ROPE] = kr0 + rot_src
    sgn[_C_KRR + A_NOPE:_C_KRR + A_NOPE + A_ROPE] = rot_sgn
    qb0 = kr0 + A_ROPE
    src[_C_QS:_C_KS] = qb0 + np.arange(SWA_COLS)
    sgn[_C_QS:_C_KS] = 1.0
    kb0 = qb0 + SWA_COLS
    vb0 = kb0 + B_KV_HEADS * B_HEAD_DIM
    for base, col0 in ((kb0, _C_KS), (vb0, _C_VS)):
        for kvh in range(B_KV_HEADS):
            for e in range(2):
                c = col0 + (2 * kvh + e) * LANES + e * B_HEAD_DIM
                src[c:c + B_HEAD_DIM] = base + kvh * B_HEAD_DIM + np.arange(B_HEAD_DIM)
                sgn[c:c + B_HEAD_DIM] = 1.0

    qd = A_NOPE + A_ROPE
    q_src = np.zeros(2 * A_HEADS * LANES, np.int32)
    q_sgn = np.zeros(2 * A_HEADS * LANES, np.float32)
    for hd in range(A_HEADS):
        lo = hd * LANES
        q_src[lo:lo + qd] = hd * qd + np.arange(qd)
        q_sgn[lo:lo + qd] = 1.0
        lo2 = A_HEADS * LANES + lo + A_NOPE
        q_src[lo2:lo2 + A_ROPE] = hd * qd + A_NOPE + rot_src
        q_sgn[lo2:lo2 + A_ROPE] = rot_sgn

    kvd = A_NOPE + A_V
    k_src = np.zeros(A_HEADS * LANES, np.int32)
    k_sgn = np.zeros(A_HEADS * LANES, np.float32)
    v_src = np.zeros(A_HEADS * V_ROWS, np.int32)
    v_sgn = np.zeros(A_HEADS * V_ROWS, np.float32)
    for hd in range(A_HEADS):
        k_src[hd * LANES:hd * LANES + A_NOPE] = hd * kvd + np.arange(A_NOPE)
        k_sgn[hd * LANES:hd * LANES + A_NOPE] = 1.0
        v_src[hd * V_ROWS:hd * V_ROWS + A_V] = hd * kvd + A_NOPE + np.arange(A_V)
        v_sgn[hd * V_ROWS:hd * V_ROWS + A_V] = 1.0
    return (src, sgn), (q_src, q_sgn), (k_src, k_sgn), (v_src, v_sgn)


def _gather_cols(w, src, sgn):
    return (jnp.take(w, jnp.asarray(src), axis=1) * jnp.asarray(sgn)[None, :]).astype(BF16)


def kernel(x, rel_bias, ffn1_pre_g, ffn1_w_gate, ffn1_w_up, ffn1_w_down, ffn1_post_g, mix_pre_g, w_in, mla_q_norm_g, mla_w_uq, mla_kv_norm_g, mla_w_ukv, swa_sink, w_out, mix_post_g, ffn2_pre_g, ffn2_w_gate, ffn2_w_up, ffn2_w_down, ffn2_post_g):
    b, s, d = x.shape
    depth = w_in.shape[0]
    assert d == D_MODEL and s % FLASH_TQ == 0 and s % SWA_TQ == 0 and s >= 3 * BLOCK
    assert (b * s) % OUT_TM == 0 and (b * s) % FFN_TM == 0

    in_l, q_l, k_l, v_l = _layout_tables()
    cq, sq, ck, sk = _rope_tables(s)
    biasm = _band_bias(rel_bias, _band_buckets())

    row = lambda g: g.reshape(1, -1)
    x2 = x.reshape(b * s, d)
    for i in range(depth):
        x2 = _ffn(x2, row(ffn1_pre_g[i]), ffn1_w_gate[i].astype(BF16), ffn1_w_up[i].astype(BF16),
                  ffn1_w_down[i].astype(BF16), row(ffn1_post_g[i]))
        win = _gather_cols(w_in[i], *in_l)
        wqt = _gather_cols(mla_w_uq[i], *q_l).T
        wk = _gather_cols(mla_w_ukv[i], *k_l)
        wvt = _gather_cols(mla_w_ukv[i], *v_l).T
        qt, k, vt, qs, ks4, vs4 = _proj(x2.reshape(b, s, d), row(mix_pre_g[i]), win,
                                        row(mla_q_norm_g[i]), wqt, row(mla_kv_norm_g[i]), wk, wvt,
                                        cq, sq, ck, sk)
        oa = _flash(qt, k, vt)
        ob = _swa(swa_sink[i], qs, ks4, vs4, biasm)
        x2 = _outproj(oa.reshape(b * s, -1), ob.reshape(b * s, -1), x2,
                      w_out[i].astype(BF16), row(mix_post_g[i]))
        x2 = _ffn(x2, row(ffn2_pre_g[i]), ffn2_w_gate[i].astype(BF16), ffn2_w_up[i].astype(BF16),
                  ffn2_w_down[i].astype(BF16), row(ffn2_post_g[i]))
    return x2.reshape(b, s, d)
```

```python
import functools
import math

import numpy as np
import jax
import jax.numpy as jnp
from jax import lax
from jax.experimental import pallas as pl
from jax.experimental.pallas import tpu as pltpu

D_MODEL = 1024
A_HEADS = 8
A_NOPE = 64
A_ROPE = 32
A_V = 64
A_Q_RANK = 384
A_KV_RANK = 256
B_HEADS = 8
B_KV_HEADS = 2
B_HEAD_DIM = 64
B_GROUP = B_HEADS // B_KV_HEADS
WINDOW = 128
BLOCK = 128
REL_BUCKETS = 32
REL_MAX_DIST = 128
D_FF = 2816
FFN_RES_WEIGHT = 0.5
ROPE_THETA = 10000.0
EPS = 1e-6
NEG_INF = -1e30

LANES = 128
V_ROWS = 80
SWA_COLS = B_HEADS * B_HEAD_DIM
VMEM_LIMIT = 56 * 1024 * 1024

FFN_TM = 512
FFN_FC = 256
PROJ_TM = 512
FLASH_TQ = 1024
FLASH_TK = PROJ_TM
SWA_TQ = 1024
OUT_TM = 1024

_C_Q = 0
_C_KV = _C_Q + A_Q_RANK
_C_KR = _C_KV + A_KV_RANK
_C_KRR = _C_KR + LANES
_C_QS = _C_KRR + LANES
_C_KS = _C_QS + SWA_COLS
_C_VS = _C_KS + 4 * LANES
N_IN = _C_VS + 4 * LANES

BF16 = jnp.bfloat16
F32 = jnp.float32


def _dot(a, b):
    return jnp.dot(a, b, preferred_element_type=F32)


def _dot_nt(a, b):
    return lax.dot_general(a, b, (((1,), (1,)), ((), ())), preferred_element_type=F32)


def _rms(x, g):
    return x * lax.rsqrt(jnp.mean(x * x, axis=-1, keepdims=True) + EPS) * g


def _const_spec(shape):
    nd = len(shape)
    return pl.BlockSpec(shape, lambda *_: (0,) * nd, pipeline_mode=pl.Buffered(1))


def _ffn_kernel(x_ref, pre_ref, wg_ref, wu_ref, wd_ref, post_ref, o_ref):
    x = x_ref[...]
    h = _rms(x, pre_ref[...]).astype(BF16)
    acc = None
    for c in range(0, D_FF, FFN_FC):
        g = _dot(h, wg_ref[:, c:c + FFN_FC])
        u = _dot(h, wu_ref[:, c:c + FFN_FC])
        a = (g * u / (1.0 + jnp.exp(-g))).astype(BF16)
        d = _dot(a, wd_ref[c:c + FFN_FC, :])
        acc = d if acc is None else acc + d
    o_ref[...] = x + FFN_RES_WEIGHT * _rms(acc, post_ref[...])


def _ffn(x2, pre_g, wg, wu, wd, post_g):
    m = x2.shape[0]
    row = pl.BlockSpec((FFN_TM, D_MODEL), lambda i: (i, 0))
    return pl.pallas_call(
        _ffn_kernel,
        grid=(m // FFN_TM,),
        in_specs=[row, _const_spec((1, D_MODEL)), _const_spec((D_MODEL, D_FF)),
                  _const_spec((D_MODEL, D_FF)), _const_spec((D_FF, D_MODEL)),
                  _const_spec((1, D_MODEL))],
        out_specs=row,
        out_shape=jax.ShapeDtypeStruct((m, D_MODEL), F32),
        compiler_params=pltpu.CompilerParams(
            dimension_semantics=("parallel",), vmem_limit_bytes=VMEM_LIMIT),
        name="ffn",
    )(x2, pre_g, wg, wu, wd, post_g)


def _proj_kernel(x_ref, pre_ref, win_ref, qn_ref, wqt_ref, kvn_ref, wk_ref, wvt_ref,
                 cq_ref, sq_ref, ck_ref, sk_ref,
                 qt_ref, k_ref, vt_ref, qs_ref, ks_ref, vs_ref):
    h = _rms(x_ref[0], pre_ref[...]).astype(BF16)
    z = _dot(h, win_ref[...])
    cqn = _rms(z[:, _C_Q:_C_KV], qn_ref[...]).astype(BF16)
    ckvn = _rms(z[:, _C_KV:_C_KR], kvn_ref[...]).astype(BF16)

    qt2 = _dot_nt(wqt_ref[...], cqn)
    half = A_HEADS * LANES
    for hd in range(A_HEADS):
        lo = hd * LANES
        qt_ref[0, hd] = (qt2[lo:lo + LANES] * cq_ref[...]
                         + qt2[half + lo:half + lo + LANES] * sq_ref[...]).astype(BF16)

    kr = z[:, _C_KR:_C_KRR] * ck_ref[...] + z[:, _C_KRR:_C_QS] * sk_ref[...]
    kn = _dot(ckvn, wk_ref[...])
    for hd in range(A_HEADS):
        lo = hd * LANES
        k_ref[0, hd] = (kn[:, lo:lo + LANES] + kr).astype(BF16)

    vt = _dot_nt(wvt_ref[...], ckvn)
    ones_row = (lax.broadcasted_iota(jnp.int32, (V_ROWS, 1), 0) == A_V).astype(F32)
    for hd in range(A_HEADS):
        lo = hd * V_ROWS
        vt_ref[0, hd, 0] = (vt[lo:lo + V_ROWS] + ones_row).astype(BF16)

    qs_ref[0] = (z[:, _C_QS:_C_KS] * (B_HEAD_DIM ** -0.5)).astype(BF16)
    ks_ref[0] = z[:, _C_KS:_C_VS].astype(BF16)
    vs_ref[0] = z[:, _C_VS:N_IN].astype(BF16)


def _proj(x3, pre_g, win, qn_g, wqt, kvn_g, wk, wvt, cq, sq, ck, sk):
    b, s, _ = x3.shape
    tm = PROJ_TM
    nt = s // tm
    grid = (b, nt)
    in_specs = [
        pl.BlockSpec((1, tm, D_MODEL), lambda i, j: (i, j, 0)),
        _const_spec((1, D_MODEL)), _const_spec((D_MODEL, N_IN)),
        _const_spec((1, A_Q_RANK)), _const_spec((2 * A_HEADS * LANES, A_Q_RANK)),
        _const_spec((1, A_KV_RANK)), _const_spec((A_KV_RANK, A_HEADS * LANES)),
        _const_spec((A_HEADS * V_ROWS, A_KV_RANK)),
        pl.BlockSpec((LANES, tm), lambda i, j: (0, j)),
        pl.BlockSpec((LANES, tm), lambda i, j: (0, j)),
        pl.BlockSpec((tm, LANES), lambda i, j: (j, 0)),
        pl.BlockSpec((tm, LANES), lambda i, j: (j, 0)),
    ]
    out_specs = [
        pl.BlockSpec((1, A_HEADS, LANES, tm), lambda i, j: (i, 0, 0, j)),
        pl.BlockSpec((1, A_HEADS, tm, LANES), lambda i, j: (i, 0, j, 0)),
        pl.BlockSpec((1, A_HEADS, 1, V_ROWS, tm), lambda i, j: (i, 0, j, 0, 0)),
        pl.BlockSpec((1, tm, SWA_COLS), lambda i, j: (i, j, 0)),
        pl.BlockSpec((1, tm, 4 * LANES), lambda i, j: (i, j, 0)),
        pl.BlockSpec((1, tm, 4 * LANES), lambda i, j: (i, j, 0)),
    ]
    out_shape = [
        jax.ShapeDtypeStruct((b, A_HEADS, LANES, s), BF16),
        jax.ShapeDtypeStruct((b, A_HEADS, s, LANES), BF16),
        jax.ShapeDtypeStruct((b, A_HEADS, nt, V_ROWS, tm), BF16),
        jax.ShapeDtypeStruct((b, s, SWA_COLS), BF16),
        jax.ShapeDtypeStruct((b, s, 4 * LANES), BF16),
        jax.ShapeDtypeStruct((b, s, 4 * LANES), BF16),
    ]
    return pl.pallas_call(
        _proj_kernel, grid=grid, in_specs=in_specs, out_specs=out_specs, out_shape=out_shape,
        compiler_params=pltpu.CompilerParams(
            dimension_semantics=("parallel", "parallel"), vmem_limit_bytes=VMEM_LIMIT),
        name="mix_proj",
    )(x3, pre_g, win, qn_g, wqt, kvn_g, wk, wvt, cq, sq, ck, sk)


def _flash_kernel(qt_ref, k_ref, vt_ref, o_ref, sa_ref, sb_ref, mca_ref, mcb_ref, m_ref, acc_ref,
                  *, nk):
    tk = FLASH_TK
    m_ref[...] = jnp.full(m_ref.shape, NEG_INF, F32)
    acc_ref[...] = jnp.zeros(acc_ref.shape, F32)

    def scores(c, s_ref, mc_ref):
        off = pl.multiple_of(c * tk, tk)
        for j in range(2):
            st = _dot(k_ref[0, j, pl.ds(off, tk), :], qt_ref[0, j])
            s_ref[j] = st
            mc_ref[j] = jnp.max(st, axis=0, keepdims=True)

    def update(c, s_ref, mc_ref):
        for j in range(2):
            m_prev = m_ref[j]
            m_new = jnp.maximum(m_prev, mc_ref[j])
            alpha = jnp.exp2(m_prev - m_new)
            p = jnp.exp2(s_ref[j] - m_new).astype(BF16)
            acc_ref[j] = alpha * acc_ref[j] + _dot(vt_ref[0, j, c], p)
            m_ref[j] = m_new

    scores(0, sa_ref, mca_ref)

    def body(i, carry):
        c = 2 * i
        scores(c + 1, sb_ref, mcb_ref)
        update(c, sa_ref, mca_ref)
        scores(c + 2, sa_ref, mca_ref)
        update(c + 1, sb_ref, mcb_ref)
        return carry

    lax.fori_loop(0, (nk - 2) // 2, body, 0)
    scores(nk - 1, sb_ref, mcb_ref)
    update(nk - 2, sa_ref, mca_ref)
    update(nk - 1, sb_ref, mcb_ref)
    outs = []
    for j in range(2):
        acc = acc_ref[j]
        outs.append(acc[:A_V] / acc[A_V:A_V + 1])
    o_ref[0] = jnp.concatenate(outs, axis=0).T.astype(BF16)


def _flash(qt, k, vt):
    b, _, _, s = qt.shape
    tq = FLASH_TQ
    nk = s // FLASH_TK
    grid = (b, A_HEADS // 2, s // tq)
    return pl.pallas_call(
        functools.partial(_flash_kernel, nk=nk),
        grid=grid,
        in_specs=[
            pl.BlockSpec((1, 2, LANES, tq), lambda i, h, j: (i, h, 0, j)),
            pl.BlockSpec((1, 2, s, LANES), lambda i, h, j: (i, h, 0, 0)),
            pl.BlockSpec((1, 2, nk, V_ROWS, FLASH_TK), lambda i, h, j: (i, h, 0, 0, 0)),
        ],
        out_specs=pl.BlockSpec((1, tq, LANES), lambda i, h, j: (i, j, h)),
        out_shape=jax.ShapeDtypeStruct((b, s, A_HEADS * A_V), BF16),
        scratch_shapes=[pltpu.VMEM((2, FLASH_TK, tq), F32), pltpu.VMEM((2, FLASH_TK, tq), F32),
                        pltpu.VMEM((2, 1, tq), F32), pltpu.VMEM((2, 1, tq), F32),
                        pltpu.VMEM((2, 1, tq), F32), pltpu.VMEM((2, V_ROWS, tq), F32)],
        compiler_params=pltpu.CompilerParams(
            dimension_semantics=("parallel", "parallel", "arbitrary"),
            vmem_limit_bytes=VMEM_LIMIT),
        name="mla_flash",
    )(qt, k, vt)


def _bias_kernel(tab_ref, bkt_ref, o_ref):
    for v in range(3):
        bkt = bkt_ref[v]
        for hd in range(B_HEADS):
            acc = jnp.full(bkt.shape, NEG_INF, F32)
            for i in range(REL_BUCKETS):
                acc = jnp.where(bkt == i, tab_ref[i, hd], acc)
            o_ref[v, hd] = acc


def _band_bias(rel_bias, buckets):
    return pl.pallas_call(
        _bias_kernel,
        in_specs=[pl.BlockSpec(memory_space=pltpu.SMEM),
                  pl.BlockSpec(memory_space=pltpu.VMEM)],
        out_specs=pl.BlockSpec(memory_space=pltpu.VMEM),
        out_shape=jax.ShapeDtypeStruct((3, B_HEADS, BLOCK, 3 * BLOCK), F32),
        name="band_bias",
    )(rel_bias, buckets)


def _swa_kernel(sink_ref, q_ref, k_ref, v_ref, bias_ref, o_ref, *, nblk, seq):
    step = pl.program_id(1)
    per_step = SWA_TQ // BLOCK
    win = 3 * BLOCK

    def blk(t, carry):
        n = step * per_step + t
        ks = pl.multiple_of(jnp.clip((n - 1) * BLOCK, 0, seq - win), BLOCK)
        var = jnp.where(n == 0, 1, jnp.where(n == nblk - 1, 2, 0))
        r0 = pl.multiple_of(t * BLOCK, BLOCK)
        for pair in range(B_HEADS // 2):
            kvh = (2 * pair) // B_GROUP
            qp = q_ref[0, pl.ds(r0, BLOCK), pair * LANES:(pair + 1) * LANES]
            acc = None
            for e in range(2):
                hd = 2 * pair + e
                c0 = (2 * kvh + e) * LANES
                kk = k_ref[0, pl.ds(ks, win), c0:c0 + LANES]
                vv = v_ref[0, pl.ds(ks, win), c0:c0 + LANES]
                s = _dot_nt(qp, kk) + bias_ref[var, hd]
                sk = sink_ref[hd]
                m = jnp.maximum(jnp.max(s, axis=1, keepdims=True), sk)
                p = jnp.exp(s - m)
                den = jnp.sum(p, axis=1, keepdims=True) + jnp.exp(sk - m)
                o = _dot(p.astype(BF16), vv) / den
                acc = o if acc is None else acc + o
            o_ref[0, pl.ds(r0, BLOCK), pair * LANES:(pair + 1) * LANES] = acc.astype(BF16)
        return carry

    lax.fori_loop(0, per_step, blk, 0)


def _swa(sink, qs, ks4, vs4, biasm):
    b, s, _ = qs.shape
    nblk = s // BLOCK
    grid = (b, s // SWA_TQ)
    return pl.pallas_call(
        functools.partial(_swa_kernel, nblk=nblk, seq=s),
        grid=grid,
        in_specs=[
            pl.BlockSpec(memory_space=pltpu.SMEM),
            pl.BlockSpec((1, SWA_TQ, SWA_COLS), lambda i, j: (i, j, 0)),
            pl.BlockSpec((1, s, 4 * LANES), lambda i, j: (i, 0, 0)),
            pl.BlockSpec((1, s, 4 * LANES), lambda i, j: (i, 0, 0)),
            _const_spec((3, B_HEADS, BLOCK, 3 * BLOCK)),
        ],
        out_specs=pl.BlockSpec((1, SWA_TQ, SWA_COLS), lambda i, j: (i, j, 0)),
        out_shape=jax.ShapeDtypeStruct((b, s, SWA_COLS), BF16),
        compiler_params=pltpu.CompilerParams(
            dimension_semantics=("parallel", "arbitrary"), vmem_limit_bytes=VMEM_LIMIT),
        name="swa",
    )(sink, qs, ks4, vs4, biasm)


def _outproj_kernel(oa_ref, ob_ref, x_ref, wo_ref, g_ref, o_ref):
    half = A_HEADS * A_V
    o = _dot(oa_ref[...], wo_ref[:half, :]) + _dot(ob_ref[...], wo_ref[half:, :])
    o_ref[...] = x_ref[...] + _rms(o, g_ref[...])


def _outproj(oa, ob, x2, wo, g):
    m = x2.shape[0]
    tm = OUT_TM
    half = A_HEADS * A_V
    return pl.pallas_call(
        _outproj_kernel,
        grid=(m // tm,),
        in_specs=[pl.BlockSpec((tm, half), lambda i: (i, 0)),
                  pl.BlockSpec((tm, SWA_COLS), lambda i: (i, 0)),
                  pl.BlockSpec((tm, D_MODEL), lambda i: (i, 0)),
                  _const_spec((D_MODEL, D_MODEL)), _const_spec((1, D_MODEL))],
        out_specs=pl.BlockSpec((tm, D_MODEL), lambda i: (i, 0)),
        out_shape=jax.ShapeDtypeStruct((m, D_MODEL), F32),
        compiler_params=pltpu.CompilerParams(
            dimension_semantics=("parallel",), vmem_limit_bytes=VMEM_LIMIT),
        name="out_proj",
    )(oa, ob, x2, wo, g)


def _t5_bucket(rel):
    nb = REL_BUCKETS // 2
    max_exact = nb // 2
    bucket = jnp.where(rel > 0, nb, 0)
    n = jnp.abs(rel)
    nf = jnp.maximum(n, 1).astype(jnp.float32)
    large = max_exact + (jnp.log(nf / max_exact) / math.log(REL_MAX_DIST / max_exact)
                         * (nb - max_exact)).astype(jnp.int32)
    large = jnp.minimum(large, nb - 1)
    return bucket + jnp.where(n < max_exact, n, large)


def _band_buckets():
    r = jnp.arange(BLOCK)[None, :, None]
    jj = jnp.arange(3 * BLOCK)[None, None, :]
    shift = jnp.array([0, BLOCK, -BLOCK])[:, None, None]
    j = jj + shift
    rel = j - BLOCK - r
    ok = (j >= 0) & (j < 3 * BLOCK) & (jnp.abs(rel) <= WINDOW)
    return jnp.where(ok, _t5_bucket(rel), -1).astype(jnp.int32)


def _rope_tables(seq):
    pos = jnp.arange(seq, dtype=jnp.float32)
    inv = ROPE_THETA ** (-jnp.arange(0, A_ROPE, 2, dtype=jnp.float32) / A_ROPE)
    ang = pos[:, None] * inv[None, :]
    cos, sin = jnp.cos(ang), jnp.sin(ang)
    cos2 = jnp.concatenate([cos, cos], axis=1)
    sin2 = jnp.concatenate([sin, sin], axis=1)
    z32 = jnp.zeros((seq, LANES - A_NOPE - A_ROPE), F32)
    ck = jnp.concatenate([jnp.zeros((seq, A_NOPE), F32), cos2, z32], axis=1)
    sk = jnp.concatenate([jnp.zeros((seq, A_NOPE), F32), sin2, z32], axis=1)
    scale = (A_NOPE + A_ROPE) ** -0.5 * math.log2(math.e)
    cq = jnp.concatenate([jnp.ones((seq, A_NOPE), F32), cos2, z32], axis=1) * scale
    return cq.T, (sk * scale).T, ck, sk


def _layout_tables():
    hr = A_ROPE // 2
    rot_src = np.concatenate([np.arange(hr, A_ROPE), np.arange(0, hr)])
    rot_sgn = np.concatenate([-np.ones(hr), np.ones(hr)])

    src = np.zeros(N_IN, np.int32)
    sgn = np.zeros(N_IN, np.float32)
    src[_C_Q:_C_KR] = np.arange(A_Q_RANK + A_KV_RANK)
    sgn[_C_Q:_C_KR] = 1.0
    kr0 = A_Q_RANK + A_KV_RANK
    src[_C_KR + A_NOPE:_C_KR + A_NOPE + A_ROPE] = kr0 + np.arange(A_ROPE)
    sgn[_C_KR + A_NOPE:_C_KR + A_NOPE + A_ROPE] = 1.0
    src[_C_KRR + A_NOPE:_C_KRR + A_NOPE + A_ROPE] = kr0 + rot_src
    sgn[_C_KRR + A_NOPE:_C_KRR + A_NOPE + A_ROPE] = rot_sgn
    qb0 = kr0 + A_ROPE
    src[_C_QS:_C_KS] = qb0 + np.arange(SWA_COLS)
    sgn[_C_QS:_C_KS] = 1.0
    kb0 = qb0 + SWA_COLS
    vb0 = kb0 + B_KV_HEADS * B_HEAD_DIM
    for base, col0 in ((kb0, _C_KS), (vb0, _C_VS)):
        for kvh in range(B_KV_HEADS):
            for e in range(2):
                c = col0 + (2 * kvh + e) * LANES + e * B_HEAD_DIM
                src[c:c + B_HEAD_DIM] = base + kvh * B_HEAD_DIM + np.arange(B_HEAD_DIM)
                sgn[c:c + B_HEAD_DIM] = 1.0

    qd = A_NOPE + A_ROPE
    q_src = np.zeros(2 * A_HEADS * LANES, np.int32)
    q_sgn = np.zeros(2 * A_HEADS * LANES, np.float32)
    for hd in range(A_HEADS):
        lo = hd * LANES
        q_src[lo:lo + qd] = hd * qd + np.arange(qd)
        q_sgn[lo:lo + qd] = 1.0
        lo2 = A_HEADS * LANES + lo + A_NOPE
        q_src[lo2:lo2 + A_ROPE] = hd * qd + A_NOPE + rot_src
        q_sgn[lo2:lo2 + A_ROPE] = rot_sgn

    kvd = A_NOPE + A_V
    k_src = np.zeros(A_HEADS * LANES, np.int32)
    k_sgn = np.zeros(A_HEADS * LANES, np.float32)
    v_src = np.zeros(A_HEADS * V_ROWS, np.int32)
    v_sgn = np.zeros(A_HEADS * V_ROWS, np.float32)
    for hd in range(A_HEADS):
        k_src[hd * LANES:hd * LANES + A_NOPE] = hd * kvd + np.arange(A_NOPE)
        k_sgn[hd * LANES:hd * LANES + A_NOPE] = 1.0
        v_src[hd * V_ROWS:hd * V_ROWS + A_V] = hd * kvd + A_NOPE + np.arange(A_V)
        v_sgn[hd * V_ROWS:hd * V_ROWS + A_V] = 1.0
    return (src, sgn), (q_src, q_sgn), (k_src, k_sgn), (v_src, v_sgn)


def _gather_cols(w, src, sgn):
    return (jnp.take(w, jnp.asarray(src), axis=1) * jnp.asarray(sgn)[None, :]).astype(BF16)


def kernel(x, rel_bias, ffn1_pre_g, ffn1_w_gate, ffn1_w_up, ffn1_w_down, ffn1_post_g, mix_pre_g, w_in, mla_q_norm_g, mla_w_uq, mla_kv_norm_g, mla_w_ukv, swa_sink, w_out, mix_post_g, ffn2_pre_g, ffn2_w_gate, ffn2_w_up, ffn2_w_down, ffn2_post_g):
    b, s, d = x.shape
    depth = w_in.shape[0]
    assert d == D_MODEL and s % FLASH_TQ == 0 and s % SWA_TQ == 0 and s >= 3 * BLOCK
    assert (s // FLASH_TK) % 2 == 0
    assert (b * s) % OUT_TM == 0 and (b * s) % FFN_TM == 0

    in_l, q_l, k_l, v_l = _layout_tables()
    cq, sq, ck, sk = _rope_tables(s)
    biasm = _band_bias(rel_bias, _band_buckets())

    row = lambda g: g.reshape(1, -1)
    x2 = x.reshape(b * s, d)
    for i in range(depth):
        x2 = _ffn(x2, row(ffn1_pre_g[i]), ffn1_w_gate[i].astype(BF16), ffn1_w_up[i].astype(BF16),
                  ffn1_w_down[i].astype(BF16), row(ffn1_post_g[i]))
        win = _gather_cols(w_in[i], *in_l)
        wqt = _gather_cols(mla_w_uq[i], *q_l).T
        wk = _gather_cols(mla_w_ukv[i], *k_l)
        wvt = _gather_cols(mla_w_ukv[i], *v_l).T
        qt, k, vt, qs, ks4, vs4 = _proj(x2.reshape(b, s, d), row(mix_pre_g[i]), win,
                                        row(mla_q_norm_g[i]), wqt, row(mla_kv_norm_g[i]), wk, wvt,
                                        cq, sq, ck, sk)
        oa = _flash(qt, k, vt)
        ob = _swa(swa_sink[i], qs, ks4, vs4, biasm)
        x2 = _outproj(oa.reshape(b * s, -1), ob.reshape(b * s, -1), x2,
                      w_out[i].astype(BF16), row(mix_post_g[i]))
        x2 = _ffn(x2, row(ffn2_pre_g[i]), ffn2_w_gate[i].astype(BF16), ffn2_w_up[i].astype(BF16),
                  ffn2_w_down[i].astype(BF16), row(ffn2_post_g[i]))
    return x2.reshape(b, s, d)
```

```python
import functools
import math

import numpy as np
import jax
import jax.numpy as jnp
from jax import lax
from jax.experimental import pallas as pl
from jax.experimental.pallas import tpu as pltpu

D_MODEL = 1024
A_HEADS = 8
A_NOPE = 64
A_ROPE = 32
A_V = 64
A_Q_RANK = 384
A_KV_RANK = 256
B_HEADS = 8
B_KV_HEADS = 2
B_HEAD_DIM = 64
B_GROUP = B_HEADS // B_KV_HEADS
WINDOW = 128
BLOCK = 128
REL_BUCKETS = 32
REL_MAX_DIST = 128
D_FF = 2816
FFN_RES_WEIGHT = 0.5
ROPE_THETA = 10000.0
EPS = 1e-6
NEG_INF = -1e30
LOG2E = math.log2(math.e)

LANES = 128
MXU_N = 256
V_ROWS = 80
SWA_COLS = B_HEADS * B_HEAD_DIM
VMEM_LIMIT = 56 * 1024 * 1024

FFN_TM = 512
FFN_FC = 256
PROJ_TM = 512
FLASH_TQ = 2048
FLASH_TK = PROJ_TM
SWA_TQ = 1024
SWA_UNROLL = 4
OUT_TM = 1024

_C_Q = 0
_C_KV = _C_Q + A_Q_RANK
_C_KR = _C_KV + A_KV_RANK
_C_KRR = _C_KR + LANES
_C_KS = _C_KRR + LANES
N_IN = _C_KS + B_KV_HEADS * B_HEAD_DIM
_R_VS = SWA_COLS
N_INT = _R_VS + B_KV_HEADS * V_ROWS

BF16 = jnp.bfloat16
F32 = jnp.float32


def _dot(a, b):
    return jnp.dot(a, b, preferred_element_type=F32)


def _dot_nt(a, b):
    return lax.dot_general(a, b, (((1,), (1,)), ((), ())), preferred_element_type=F32)


def _rms(x, g):
    return x * lax.rsqrt(jnp.mean(x * x, axis=-1, keepdims=True) + EPS) * g


def _const_spec(shape):
    nd = len(shape)
    return pl.BlockSpec(shape, lambda *_: (0,) * nd, pipeline_mode=pl.Buffered(1))


def _ffn_kernel(x_ref, pre_ref, wg_ref, wu_ref, wd_ref, post_ref, o_ref):
    x = x_ref[...]
    h = _rms(x, pre_ref[...]).astype(BF16)
    acc = None
    for c in range(0, D_FF, FFN_FC):
        g = _dot(h, wg_ref[:, c:c + FFN_FC])
        u = _dot(h, wu_ref[:, c:c + FFN_FC])
        a = (g * u / (1.0 + jnp.exp(-g))).astype(BF16)
        d = _dot(a, wd_ref[c:c + FFN_FC, :])
        acc = d if acc is None else acc + d
    o_ref[...] = x + FFN_RES_WEIGHT * _rms(acc, post_ref[...])


def _ffn(x2, pre_g, wg, wu, wd, post_g):
    m = x2.shape[0]
    row = pl.BlockSpec((FFN_TM, D_MODEL), lambda i: (i, 0))
    return pl.pallas_call(
        _ffn_kernel,
        grid=(m // FFN_TM,),
        in_specs=[row, _const_spec((1, D_MODEL)), _const_spec((D_MODEL, D_FF)),
                  _const_spec((D_MODEL, D_FF)), _const_spec((D_FF, D_MODEL)),
                  _const_spec((1, D_MODEL))],
        out_specs=row,
        out_shape=jax.ShapeDtypeStruct((m, D_MODEL), F32),
        compiler_params=pltpu.CompilerParams(
            dimension_semantics=("parallel",), vmem_limit_bytes=VMEM_LIMIT),
        name="ffn",
    )(x2, pre_g, wg, wu, wd, post_g)


def _proj_kernel(x_ref, pre_ref, win_ref, wint_ref, qn_ref, wqt_ref, kvn_ref, wk_ref, wvt_ref,
                 cq_ref, sq_ref, ck_ref, sk_ref,
                 qt_ref, k_ref, vt_ref, qst_ref, ks_ref, vst_ref):
    h = _rms(x_ref[0], pre_ref[...]).astype(BF16)
    z = _dot(h, win_ref[...])
    zt = _dot_nt(wint_ref[...], h)
    cqn = _rms(z[:, _C_Q:_C_KV], qn_ref[...]).astype(BF16)
    ckvn = _rms(z[:, _C_KV:_C_KR], kvn_ref[...]).astype(BF16)

    qt2 = _dot_nt(wqt_ref[...], cqn)
    half = A_HEADS * LANES
    for hd in range(A_HEADS):
        lo = hd * LANES
        qt_ref[0, hd] = (qt2[lo:lo + LANES] * cq_ref[...]
                         + qt2[half + lo:half + lo + LANES] * sq_ref[...]).astype(BF16)

    kr = z[:, _C_KR:_C_KRR] * ck_ref[...] + z[:, _C_KRR:_C_KS] * sk_ref[...]
    kn = _dot(ckvn, wk_ref[...])
    for hd in range(A_HEADS):
        lo = hd * LANES
        k_ref[0, hd] = (kn[:, lo:lo + LANES] + kr).astype(BF16)

    vt = _dot_nt(wvt_ref[...], ckvn)
    ones_row = (lax.broadcasted_iota(jnp.int32, (V_ROWS, 1), 0) == A_V).astype(F32)
    for hd in range(A_HEADS):
        lo = hd * V_ROWS
        vt_ref[0, hd, 0] = (vt[lo:lo + V_ROWS] + ones_row).astype(BF16)

    ks_ref[0] = z[:, _C_KS:N_IN].astype(BF16)
    qall = zt[:_R_VS] * (B_HEAD_DIM ** -0.5 * LOG2E)
    zero = jnp.zeros((B_HEAD_DIM, B_GROUP * BLOCK), F32)
    for blk in range(x_ref.shape[1] // BLOCK):
        cols = slice(blk * BLOCK, (blk + 1) * BLOCK)
        for kvh in range(B_KV_HEADS):
            piece = jnp.concatenate(
                [qall[(kvh * B_GROUP + g) * B_HEAD_DIM:(kvh * B_GROUP + g + 1) * B_HEAD_DIM, cols]
                 for g in range(B_GROUP)], axis=1)
            parts = [piece if i == kvh else zero for i in range(B_KV_HEADS)]
            qst_ref[0, blk, kvh] = jnp.concatenate(parts, axis=0).astype(BF16)
            lo = _R_VS + kvh * V_ROWS
            vst_ref[0, blk, kvh] = (zt[lo:lo + V_ROWS, cols] + ones_row).astype(BF16)


def _proj(x3, pre_g, win, wint, qn_g, wqt, kvn_g, wk, wvt, cq, sq, ck, sk):
    b, s, _ = x3.shape
    tm = PROJ_TM
    nt = s // tm
    nb = tm // BLOCK
    grid = (b, nt)
    in_specs = [
        pl.BlockSpec((1, tm, D_MODEL), lambda i, j: (i, j, 0)),
        _const_spec((1, D_MODEL)), _const_spec((D_MODEL, N_IN)), _const_spec((N_INT, D_MODEL)),
        _const_spec((1, A_Q_RANK)), _const_spec((2 * A_HEADS * LANES, A_Q_RANK)),
        _const_spec((1, A_KV_RANK)), _const_spec((A_KV_RANK, A_HEADS * LANES)),
        _const_spec((A_HEADS * V_ROWS, A_KV_RANK)),
        pl.BlockSpec((LANES, tm), lambda i, j: (0, j)),
        pl.BlockSpec((LANES, tm), lambda i, j: (0, j)),
        pl.BlockSpec((tm, LANES), lambda i, j: (j, 0)),
        pl.BlockSpec((tm, LANES), lambda i, j: (j, 0)),
    ]
    out_specs = [
        pl.BlockSpec((1, A_HEADS, LANES, tm), lambda i, j: (i, 0, 0, j)),
        pl.BlockSpec((1, A_HEADS, tm, LANES), lambda i, j: (i, 0, j, 0)),
        pl.BlockSpec((1, A_HEADS, 1, V_ROWS, tm), lambda i, j: (i, 0, j, 0, 0)),
        pl.BlockSpec((1, nb, B_KV_HEADS, LANES, B_GROUP * BLOCK), lambda i, j: (i, j, 0, 0, 0)),
        pl.BlockSpec((1, tm, LANES), lambda i, j: (i, j, 0)),
        pl.BlockSpec((1, nb, B_KV_HEADS, V_ROWS, BLOCK), lambda i, j: (i, j, 0, 0, 0)),
    ]
    out_shape = [
        jax.ShapeDtypeStruct((b, A_HEADS, LANES, s), BF16),
        jax.ShapeDtypeStruct((b, A_HEADS, s, LANES), BF16),
        jax.ShapeDtypeStruct((b, A_HEADS, nt, V_ROWS, tm), BF16),
        jax.ShapeDtypeStruct((b, s // BLOCK, B_KV_HEADS, LANES, B_GROUP * BLOCK), BF16),
        jax.ShapeDtypeStruct((b, s, LANES), BF16),
        jax.ShapeDtypeStruct((b, s // BLOCK, B_KV_HEADS, V_ROWS, BLOCK), BF16),
    ]
    return pl.pallas_call(
        _proj_kernel, grid=grid, in_specs=in_specs, out_specs=out_specs, out_shape=out_shape,
        compiler_params=pltpu.CompilerParams(
            dimension_semantics=("parallel", "parallel"), vmem_limit_bytes=VMEM_LIMIT),
        name="mix_proj",
    )(x3, pre_g, win, wint, qn_g, wqt, kvn_g, wk, wvt, cq, sq, ck, sk)


def _flash_kernel(qt_ref, k_ref, vt_ref, o_ref, sa_ref, sb_ref, mca_ref, mcb_ref, m_ref, acc_ref,
                  *, nk):
    tk = FLASH_TK
    m_ref[...] = jnp.full(m_ref.shape, NEG_INF, F32)
    acc_ref[...] = jnp.zeros(acc_ref.shape, F32)

    tiles = [(j, slice(n * MXU_N, (n + 1) * MXU_N))
             for j in range(2) for n in range(qt_ref.shape[3] // MXU_N)]

    def score_tile(c, s_ref, mc_ref, j, ln):
        off = pl.multiple_of(c * tk, tk)
        st = _dot(k_ref[0, j, pl.ds(off, tk), :], qt_ref[0, j, :, ln])
        s_ref[j, :, ln] = st
        mc_ref[j, :, ln] = jnp.max(st, axis=0, keepdims=True)

    def update_tile(c, s_ref, mc_ref, j, ln):
        m_prev = m_ref[j, :, ln]
        m_new = jnp.maximum(m_prev, mc_ref[j, :, ln])
        alpha = jnp.exp2(m_prev - m_new)
        p = jnp.exp2(s_ref[j, :, ln] - m_new).astype(BF16)
        acc_ref[j, :, ln] = alpha * acc_ref[j, :, ln] + _dot(vt_ref[0, j, c], p)
        m_ref[j, :, ln] = m_new

    def scores(c, s_ref, mc_ref):
        for t in tiles:
            score_tile(c, s_ref, mc_ref, *t)

    def update(c, s_ref, mc_ref):
        for t in tiles:
            update_tile(c, s_ref, mc_ref, *t)

    def step(c_next, s_next, mc_next, c_cur, s_cur, mc_cur):
        score_tile(c_next, s_next, mc_next, *tiles[0])
        for i, t in enumerate(tiles):
            if i + 1 < len(tiles):
                score_tile(c_next, s_next, mc_next, *tiles[i + 1])
            update_tile(c_cur, s_cur, mc_cur, *t)

    scores(0, sa_ref, mca_ref)

    def body(i, carry):
        c = 2 * i
        step(c + 1, sb_ref, mcb_ref, c, sa_ref, mca_ref)
        step(c + 2, sa_ref, mca_ref, c + 1, sb_ref, mcb_ref)
        return carry

    lax.fori_loop(0, (nk - 2) // 2, body, 0)
    step(nk - 1, sb_ref, mcb_ref, nk - 2, sa_ref, mca_ref)
    update(nk - 1, sb_ref, mcb_ref)
    outs = []
    for j in range(2):
        acc = acc_ref[j]
        outs.append(acc[:A_V] / acc[A_V:A_V + 1])
    o_ref[0] = jnp.concatenate(outs, axis=0).T.astype(BF16)


def _flash(qt, k, vt):
    b, _, _, s = qt.shape
    tq = FLASH_TQ
    nk = s // FLASH_TK
    grid = (b, A_HEADS // 2, s // tq)
    return pl.pallas_call(
        functools.partial(_flash_kernel, nk=nk),
        grid=grid,
        in_specs=[
            pl.BlockSpec((1, 2, LANES, tq), lambda i, h, j: (i, h, 0, j)),
            pl.BlockSpec((1, 2, s, LANES), lambda i, h, j: (i, h, 0, 0)),
            pl.BlockSpec((1, 2, nk, V_ROWS, FLASH_TK), lambda i, h, j: (i, h, 0, 0, 0)),
        ],
        out_specs=pl.BlockSpec((1, tq, LANES), lambda i, h, j: (i, j, h)),
        out_shape=jax.ShapeDtypeStruct((b, s, A_HEADS * A_V), BF16),
        scratch_shapes=[pltpu.VMEM((2, FLASH_TK, tq), F32), pltpu.VMEM((2, FLASH_TK, tq), F32),
                        pltpu.VMEM((2, 1, tq), F32), pltpu.VMEM((2, 1, tq), F32),
                        pltpu.VMEM((2, 1, tq), F32), pltpu.VMEM((2, V_ROWS, tq), F32)],
        compiler_params=pltpu.CompilerParams(
            dimension_semantics=("parallel", "parallel", "arbitrary"),
            vmem_limit_bytes=VMEM_LIMIT),
        name="mla_flash",
    )(qt, k, vt)


def _bias_kernel(tab_ref, bkt_ref, o_ref):
    for v in range(3):
        bkt = bkt_ref[v]
        for hd in range(B_HEADS):
            acc = jnp.full(bkt.shape, NEG_INF, F32)
            for i in range(REL_BUCKETS):
                acc = jnp.where(bkt == i, tab_ref[i, hd] * LOG2E, acc)
            kvh, g = divmod(hd, B_GROUP)
            o_ref[v, kvh, :, g * BLOCK:(g + 1) * BLOCK] = acc


def _band_bias(rel_bias, buckets):
    return pl.pallas_call(
        _bias_kernel,
        in_specs=[pl.BlockSpec(memory_space=pltpu.SMEM),
                  pl.BlockSpec(memory_space=pltpu.VMEM)],
        out_specs=pl.BlockSpec(memory_space=pltpu.VMEM),
        out_shape=jax.ShapeDtypeStruct((3, B_KV_HEADS, 3 * BLOCK, B_GROUP * BLOCK), F32),
        name="band_bias",
    )(rel_bias, buckets)


def _swa_kernel(sink_ref, qt_ref, k_ref, vt_ref, bias_ref, o_ref, *, nblk):
    step = pl.program_id(1)
    per_step = SWA_TQ // BLOCK
    lane_head = lax.shift_right_logical(
        lax.broadcasted_iota(jnp.int32, (1, B_GROUP * BLOCK), 1), int(math.log2(BLOCK)))
    sinks = []
    for kvh in range(B_KV_HEADS):
        sv = jnp.zeros((1, B_GROUP * BLOCK), F32)
        for g in range(B_GROUP):
            sv = jnp.where(lane_head == g, sink_ref[kvh * B_GROUP + g] * LOG2E, sv)
        sinks.append(sv)

    def blk(t, carry):
        n = step * per_step + t
        nb0 = jnp.clip(n - 1, 0, nblk - 3)
        var = jnp.where(n == 0, 1, jnp.where(n == nblk - 1, 2, 0))
        kwin = k_ref[0, pl.ds(pl.multiple_of(nb0 * BLOCK, BLOCK), 3 * BLOCK), :]
        rows = []
        for kvh in range(B_KV_HEADS):
            st = _dot(kwin, qt_ref[0, t, kvh]) + bias_ref[var, kvh]
            m = jnp.maximum(jnp.max(st, axis=0, keepdims=True), sinks[kvh])
            p = jnp.exp2(st - m).astype(BF16)
            acc = None
            for w in range(3):
                d = _dot(vt_ref[0, nb0 + w, kvh], p[w * BLOCK:(w + 1) * BLOCK])
                acc = d if acc is None else acc + d
            den = acc[B_HEAD_DIM:B_HEAD_DIM + 1] + jnp.exp2(sinks[kvh] - m)
            out = acc[:B_HEAD_DIM] / den
            for g in range(B_GROUP):
                rows.append(out[:, g * BLOCK:(g + 1) * BLOCK])
        ot = jnp.concatenate(rows, axis=0)
        r0 = pl.multiple_of(t * BLOCK, BLOCK)
        o_ref[0, pl.ds(r0, BLOCK), :] = ot.T.astype(BF16)
        return carry

    lax.fori_loop(0, per_step, blk, 0, unroll=SWA_UNROLL)


def _swa(sink, qst, ks, vst, biast):
    b, nblk = qst.shape[0], qst.shape[1]
    s = nblk * BLOCK
    per_step = SWA_TQ // BLOCK
    grid = (b, s // SWA_TQ)
    return pl.pallas_call(
        functools.partial(_swa_kernel, nblk=nblk),
        grid=grid,
        in_specs=[
            pl.BlockSpec(memory_space=pltpu.SMEM),
            pl.BlockSpec((1, per_step, B_KV_HEADS, LANES, B_GROUP * BLOCK),
                         lambda i, j: (i, j, 0, 0, 0)),
            pl.BlockSpec((1, s, LANES), lambda i, j: (i, 0, 0)),
            pl.BlockSpec((1, nblk, B_KV_HEADS, V_ROWS, BLOCK), lambda i, j: (i, 0, 0, 0, 0)),
            _const_spec((3, B_KV_HEADS, 3 * BLOCK, B_GROUP * BLOCK)),
        ],
        out_specs=pl.BlockSpec((1, SWA_TQ, SWA_COLS), lambda i, j: (i, j, 0)),
        out_shape=jax.ShapeDtypeStruct((b, s, SWA_COLS), BF16),
        compiler_params=pltpu.CompilerParams(
            dimension_semantics=("parallel", "arbitrary"), vmem_limit_bytes=VMEM_LIMIT),
        name="swa",
    )(sink, qst, ks, vst, biast)


def _outproj_kernel(oa_ref, ob_ref, x_ref, wo_ref, g_ref, o_ref):
    half = A_HEADS * A_V
    o = _dot(oa_ref[...], wo_ref[:half, :]) + _dot(ob_ref[...], wo_ref[half:, :])
    o_ref[...] = x_ref[...] + _rms(o, g_ref[...])


def _outproj(oa, ob, x2, wo, g):
    m = x2.shape[0]
    tm = OUT_TM
    half = A_HEADS * A_V
    return pl.pallas_call(
        _outproj_kernel,
        grid=(m // tm,),
        in_specs=[pl.BlockSpec((tm, half), lambda i: (i, 0)),
                  pl.BlockSpec((tm, SWA_COLS), lambda i: (i, 0)),
                  pl.BlockSpec((tm, D_MODEL), lambda i: (i, 0)),
                  _const_spec((D_MODEL, D_MODEL)), _const_spec((1, D_MODEL))],
        out_specs=pl.BlockSpec((tm, D_MODEL), lambda i: (i, 0)),
        out_shape=jax.ShapeDtypeStruct((m, D_MODEL), F32),
        compiler_params=pltpu.CompilerParams(
            dimension_semantics=("parallel",), vmem_limit_bytes=VMEM_LIMIT),
        name="out_proj",
    )(oa, ob, x2, wo, g)


def _t5_bucket(rel):
    nb = REL_BUCKETS // 2
    max_exact = nb // 2
    bucket = jnp.where(rel > 0, nb, 0)
    n = jnp.abs(rel)
    nf = jnp.maximum(n, 1).astype(jnp.float32)
    large = max_exact + (jnp.log(nf / max_exact) / math.log(REL_MAX_DIST / max_exact)
                         * (nb - max_exact)).astype(jnp.int32)
    large = jnp.minimum(large, nb - 1)
    return bucket + jnp.where(n < max_exact, n, large)


def _band_buckets():
    jj = jnp.arange(3 * BLOCK)[None, :, None]
    r = jnp.arange(BLOCK)[None, None, :]
    shift = jnp.array([0, BLOCK, -BLOCK])[:, None, None]
    j = jj + shift
    rel = j - BLOCK - r
    ok = (j >= 0) & (j < 3 * BLOCK) & (jnp.abs(rel) <= WINDOW)
    return jnp.where(ok, _t5_bucket(rel), -1).astype(jnp.int32)


def _rope_tables(seq):
    pos = jnp.arange(seq, dtype=jnp.float32)
    inv = ROPE_THETA ** (-jnp.arange(0, A_ROPE, 2, dtype=jnp.float32) / A_ROPE)
    ang = pos[:, None] * inv[None, :]
    cos, sin = jnp.cos(ang), jnp.sin(ang)
    cos2 = jnp.concatenate([cos, cos], axis=1)
    sin2 = jnp.concatenate([sin, sin], axis=1)
    z32 = jnp.zeros((seq, LANES - A_NOPE - A_ROPE), F32)
    ck = jnp.concatenate([jnp.zeros((seq, A_NOPE), F32), cos2, z32], axis=1)
    sk = jnp.concatenate([jnp.zeros((seq, A_NOPE), F32), sin2, z32], axis=1)
    scale = (A_NOPE + A_ROPE) ** -0.5 * LOG2E
    cq = jnp.concatenate([jnp.ones((seq, A_NOPE), F32), cos2, z32], axis=1) * scale
    return cq.T, (sk * scale).T, ck, sk


def _layout_tables():
    hr = A_ROPE // 2
    rot_src = np.concatenate([np.arange(hr, A_ROPE), np.arange(0, hr)])
    rot_sgn = np.concatenate([-np.ones(hr), np.ones(hr)])

    kr0 = A_Q_RANK + A_KV_RANK
    qb0 = kr0 + A_ROPE
    kb0 = qb0 + SWA_COLS
    vb0 = kb0 + B_KV_HEADS * B_HEAD_DIM

    src = np.zeros(N_IN, np.int32)
    sgn = np.zeros(N_IN, np.float32)
    src[_C_Q:_C_KR] = np.arange(kr0)
    sgn[_C_Q:_C_KR] = 1.0
    src[_C_KR + A_NOPE:_C_KR + A_NOPE + A_ROPE] = kr0 + np.arange(A_ROPE)
    sgn[_C_KR + A_NOPE:_C_KR + A_NOPE + A_ROPE] = 1.0
    src[_C_KRR + A_NOPE:_C_KRR + A_NOPE + A_ROPE] = kr0 + rot_src
    sgn[_C_KRR + A_NOPE:_C_KRR + A_NOPE + A_ROPE] = rot_sgn
    src[_C_KS:N_IN] = kb0 + np.arange(B_KV_HEADS * B_HEAD_DIM)
    sgn[_C_KS:N_IN] = 1.0

    t_src = np.zeros(N_INT, np.int32)
    t_sgn = np.zeros(N_INT, np.float32)
    t_src[:_R_VS] = qb0 + np.arange(SWA_COLS)
    t_sgn[:_R_VS] = 1.0
    for kvh in range(B_KV_HEADS):
        lo = _R_VS + kvh * V_ROWS
        t_src[lo:lo + B_HEAD_DIM] = vb0 + kvh * B_HEAD_DIM + np.arange(B_HEAD_DIM)
        t_sgn[lo:lo + B_HEAD_DIM] = 1.0

    qd = A_NOPE + A_ROPE
    q_src = np.zeros(2 * A_HEADS * LANES, np.int32)
    q_sgn = np.zeros(2 * A_HEADS * LANES, np.float32)
    for hd in range(A_HEADS):
        lo = hd * LANES
        q_src[lo:lo + qd] = hd * qd + np.arange(qd)
        q_sgn[lo:lo + qd] = 1.0
        lo2 = A_HEADS * LANES + lo + A_NOPE
        q_src[lo2:lo2 + A_ROPE] = hd * qd + A_NOPE + rot_src
        q_sgn[lo2:lo2 + A_ROPE] = rot_sgn

    kvd = A_NOPE + A_V
    k_src = np.zeros(A_HEADS * LANES, np.int32)
    k_sgn = np.zeros(A_HEADS * LANES, np.float32)
    v_src = np.zeros(A_HEADS * V_ROWS, np.int32)
    v_sgn = np.zeros(A_HEADS * V_ROWS, np.float32)
    for hd in range(A_HEADS):
        k_src[hd * LANES:hd * LANES + A_NOPE] = hd * kvd + np.arange(A_NOPE)
        k_sgn[hd * LANES:hd * LANES + A_NOPE] = 1.0
        v_src[hd * V_ROWS:hd * V_ROWS + A_V] = hd * kvd + A_NOPE + np.arange(A_V)
        v_sgn[hd * V_ROWS:hd * V_ROWS + A_V] = 1.0
    return (src, sgn), (t_src, t_sgn), (q_src, q_sgn), (k_src, k_sgn), (v_src, v_sgn)


def _gather_cols(w, src, sgn):
    return (jnp.take(w, jnp.asarray(src), axis=1) * jnp.asarray(sgn)[None, :]).astype(BF16)


def kernel(x, rel_bias, ffn1_pre_g, ffn1_w_gate, ffn1_w_up, ffn1_w_down, ffn1_post_g, mix_pre_g, w_in, mla_q_norm_g, mla_w_uq, mla_kv_norm_g, mla_w_ukv, swa_sink, w_out, mix_post_g, ffn2_pre_g, ffn2_w_gate, ffn2_w_up, ffn2_w_down, ffn2_post_g):
    b, s, d = x.shape
    depth = w_in.shape[0]
    assert d == D_MODEL and s % FLASH_TQ == 0 and s % SWA_TQ == 0 and s >= 3 * BLOCK
    assert (s // FLASH_TK) % 2 == 0
    assert (b * s) % OUT_TM == 0 and (b * s) % FFN_TM == 0

    in_l, int_l, q_l, k_l, v_l = _layout_tables()
    cq, sq, ck, sk = _rope_tables(s)
    biast = _band_bias(rel_bias, _band_buckets())

    row = lambda g: g.reshape(1, -1)
    x2 = x.reshape(b * s, d)
    for i in range(depth):
        x2 = _ffn(x2, row(ffn1_pre_g[i]), ffn1_w_gate[i].astype(BF16), ffn1_w_up[i].astype(BF16),
                  ffn1_w_down[i].astype(BF16), row(ffn1_post_g[i]))
        win = _gather_cols(w_in[i], *in_l)
        wint = _gather_cols(w_in[i], *int_l).T
        wqt = _gather_cols(mla_w_uq[i], *q_l).T
        wk = _gather_cols(mla_w_ukv[i], *k_l)
        wvt = _gather_cols(mla_w_ukv[i], *v_l).T
        qt, k, vt, qst, ks, vst = _proj(x2.reshape(b, s, d), row(mix_pre_g[i]), win, wint,
                                        row(mla_q_norm_g[i]), wqt, row(mla_kv_norm_g[i]), wk, wvt,
                                        cq, sq, ck, sk)
        oa = _flash(qt, k, vt)
        ob = _swa(swa_sink[i], qst, ks, vst, biast)
        x2 = _outproj(oa.reshape(b * s, -1), ob.reshape(b * s, -1), x2,
                      w_out[i].astype(BF16), row(mix_post_g[i]))
        x2 = _ffn(x2, row(ffn2_pre_g[i]), ffn2_w_gate[i].astype(BF16), ffn2_w_up[i].astype(BF16),
                  ffn2_w_down[i].astype(BF16), row(ffn2_post_g[i]))
    return x2.reshape(b, s, d)
```

```python
import functools
import math

import numpy as np
import jax
import jax.numpy as jnp
from jax import lax
from jax.experimental import pallas as pl
from jax.experimental.pallas import tpu as pltpu

D_MODEL = 1024
A_HEADS = 8
A_NOPE = 64
A_ROPE = 32
A_V = 64
A_Q_RANK = 384
A_KV_RANK = 256
B_HEADS = 8
B_KV_HEADS = 2
B_HEAD_DIM = 64
B_GROUP = B_HEADS // B_KV_HEADS
WINDOW = 128
BLOCK = 128
REL_BUCKETS = 32
REL_MAX_DIST = 128
D_FF = 2816
FFN_RES_WEIGHT = 0.5
ROPE_THETA = 10000.0
EPS = 1e-6
NEG_INF = -1e30
LOG2E = math.log2(math.e)

LANES = 128
MXU_N = 256
V_ROWS = 80
SWA_COLS = B_HEADS * B_HEAD_DIM
VMEM_LIMIT = 56 * 1024 * 1024

FFN_TM = 512
FFN_FC = 256
PROJ_TM = 512
FLASH_TQ = 2048
FLASH_TK = 256
FLASH_STEPS = 4
SWA_TQ = 1024
SWA_UNROLL = 4
SWA_LEAD = 2

_C_Q = 0
_C_KV = _C_Q + A_Q_RANK
_C_KR = _C_KV + A_KV_RANK
_C_KRR = _C_KR + LANES
_C_KS = _C_KRR + LANES
N_IN = _C_KS + B_KV_HEADS * B_HEAD_DIM
_R_VS = SWA_COLS
N_INT = _R_VS + B_KV_HEADS * V_ROWS

BF16 = jnp.bfloat16
F32 = jnp.float32


def _dot(a, b):
    return jnp.dot(a, b, preferred_element_type=F32)


def _dot_nt(a, b):
    return lax.dot_general(a, b, (((1,), (1,)), ((), ())), preferred_element_type=F32)


def _rms(x, g):
    return x * lax.rsqrt(jnp.mean(x * x, axis=-1, keepdims=True) + EPS) * g


def _const_spec(shape):
    nd = len(shape)
    return pl.BlockSpec(shape, lambda *_: (0,) * nd, pipeline_mode=pl.Buffered(1))


def _ffn_kernel(*refs, mixer):
    if mixer:
        oa_ref, ob_ref, wo_ref, mg_ref, x_ref, pre_ref, wg_ref, wu_ref, wd_ref, post_ref, o_ref = refs
        half = A_HEADS * A_V
        o = _dot(oa_ref[...], wo_ref[:half, :]) + _dot(ob_ref[...], wo_ref[half:, :])
        x = x_ref[...] + _rms(o, mg_ref[...])
    else:
        x_ref, pre_ref, wg_ref, wu_ref, wd_ref, post_ref, o_ref = refs
        x = x_ref[...]
    h = _rms(x, pre_ref[...]).astype(BF16)
    acc = None
    for c in range(0, D_FF, FFN_FC):
        g = _dot(h, wg_ref[:, c:c + FFN_FC])
        u = _dot(h, wu_ref[:, c:c + FFN_FC])
        a = (g * u / (1.0 + jnp.exp(-g))).astype(BF16)
        d = _dot(a, wd_ref[c:c + FFN_FC, :])
        acc = d if acc is None else acc + d
    o_ref[...] = x + FFN_RES_WEIGHT * _rms(acc, post_ref[...])


def _ffn(x2, pre_g, wg, wu, wd, post_g, mixer=None):
    m = x2.shape[0]
    row = pl.BlockSpec((FFN_TM, D_MODEL), lambda i: (i, 0))
    in_specs = [row, _const_spec((1, D_MODEL)), _const_spec((D_MODEL, D_FF)),
                _const_spec((D_MODEL, D_FF)), _const_spec((D_FF, D_MODEL)),
                _const_spec((1, D_MODEL))]
    args = (x2, pre_g, wg, wu, wd, post_g)
    if mixer is not None:
        half = pl.BlockSpec((FFN_TM, mixer[0].shape[1]), lambda i: (i, 0))
        in_specs = [half, half, _const_spec((D_MODEL, D_MODEL)), _const_spec((1, D_MODEL))] + in_specs
        args = tuple(mixer) + args
    return pl.pallas_call(
        functools.partial(_ffn_kernel, mixer=mixer is not None),
        grid=(m // FFN_TM,),
        in_specs=in_specs,
        out_specs=row,
        out_shape=jax.ShapeDtypeStruct((m, D_MODEL), F32),
        compiler_params=pltpu.CompilerParams(
            dimension_semantics=("parallel",), vmem_limit_bytes=VMEM_LIMIT),
        name="mix_ffn" if mixer is not None else "ffn",
    )(*args)


def _proj_kernel(x_ref, pre_ref, win_ref, wint_ref, qn_ref, wqt_ref, kvn_ref, wk_ref, wvt_ref,
                 cq_ref, sq_ref, ck_ref, sk_ref,
                 qt_ref, k_ref, vt_ref, qst_ref, ks_ref, vst_ref):
    h = _rms(x_ref[0], pre_ref[...]).astype(BF16)
    z = _dot(h, win_ref[...])
    zt = _dot_nt(wint_ref[...], h)
    cqn = _rms(z[:, _C_Q:_C_KV], qn_ref[...]).astype(BF16)
    ckvn = _rms(z[:, _C_KV:_C_KR], kvn_ref[...]).astype(BF16)

    qt2 = _dot_nt(wqt_ref[...], cqn)
    r0, r1 = A_NOPE, A_NOPE + A_ROPE
    cq = cq_ref[...]
    sq = sq_ref[...]
    pad = jnp.zeros((LANES - r1, x_ref.shape[1]), F32)
    for hd in range(A_HEADS):
        blk = qt2[hd * LANES:(hd + 1) * LANES]
        rope = blk[r0:r1] * cq[r0:r1] + blk[r1:] * sq[r0:r1]
        qt_ref[0, hd] = jnp.concatenate([blk[:r0] * cq[:r0], rope, pad], axis=0).astype(BF16)

    kr = z[:, _C_KR:_C_KRR] * ck_ref[...] + z[:, _C_KRR:_C_KS] * sk_ref[...]
    kn = _dot(ckvn, wk_ref[...])
    for hd in range(A_HEADS):
        lo = hd * LANES
        k_ref[0, hd] = (kn[:, lo:lo + LANES] + kr).astype(BF16)

    vt = _dot_nt(wvt_ref[...], ckvn)
    ones_row = (lax.broadcasted_iota(jnp.int32, (V_ROWS, 1), 0) == A_V).astype(F32)
    for hd in range(A_HEADS):
        lo = hd * V_ROWS
        vt_ref[0, hd, 0] = (vt[lo:lo + V_ROWS] + ones_row).astype(BF16)

    ks_ref[0] = z[:, _C_KS:N_IN].astype(BF16)
    qall = zt[:_R_VS] * (B_HEAD_DIM ** -0.5 * LOG2E)
    zero = jnp.zeros((B_HEAD_DIM, B_GROUP * BLOCK), F32)
    for blk in range(x_ref.shape[1] // BLOCK):
        cols = slice(blk * BLOCK, (blk + 1) * BLOCK)
        for kvh in range(B_KV_HEADS):
            piece = jnp.concatenate(
                [qall[(kvh * B_GROUP + g) * B_HEAD_DIM:(kvh * B_GROUP + g + 1) * B_HEAD_DIM, cols]
                 for g in range(B_GROUP)], axis=1)
            parts = [piece if i == kvh else zero for i in range(B_KV_HEADS)]
            qst_ref[0, blk, kvh] = jnp.concatenate(parts, axis=0).astype(BF16)
            lo = _R_VS + kvh * V_ROWS
            vst_ref[0, blk, kvh] = (zt[lo:lo + V_ROWS, cols] + ones_row).astype(BF16)


def _proj(x3, pre_g, win, wint, qn_g, wqt, kvn_g, wk, wvt, cq, sq, ck, sk):
    b, s, _ = x3.shape
    tm = PROJ_TM
    nt = s // tm
    nb = tm // BLOCK
    grid = (b, nt)
    in_specs = [
        pl.BlockSpec((1, tm, D_MODEL), lambda i, j: (i, j, 0)),
        _const_spec((1, D_MODEL)), _const_spec((D_MODEL, N_IN)), _const_spec((N_INT, D_MODEL)),
        _const_spec((1, A_Q_RANK)), _const_spec((A_HEADS * LANES, A_Q_RANK)),
        _const_spec((1, A_KV_RANK)), _const_spec((A_KV_RANK, A_HEADS * LANES)),
        _const_spec((A_HEADS * V_ROWS, A_KV_RANK)),
        pl.BlockSpec((LANES, tm), lambda i, j: (0, j)),
        pl.BlockSpec((LANES, tm), lambda i, j: (0, j)),
        pl.BlockSpec((tm, LANES), lambda i, j: (j, 0)),
        pl.BlockSpec((tm, LANES), lambda i, j: (j, 0)),
    ]
    out_specs = [
        pl.BlockSpec((1, A_HEADS, LANES, tm), lambda i, j: (i, 0, 0, j)),
        pl.BlockSpec((1, A_HEADS, tm, LANES), lambda i, j: (i, 0, j, 0)),
        pl.BlockSpec((1, A_HEADS, 1, V_ROWS, tm), lambda i, j: (i, 0, j, 0, 0)),
        pl.BlockSpec((1, nb, B_KV_HEADS, LANES, B_GROUP * BLOCK), lambda i, j: (i, j, 0, 0, 0)),
        pl.BlockSpec((1, tm, LANES), lambda i, j: (i, j, 0)),
        pl.BlockSpec((1, nb, B_KV_HEADS, V_ROWS, BLOCK), lambda i, j: (i, j, 0, 0, 0)),
    ]
    out_shape = [
        jax.ShapeDtypeStruct((b, A_HEADS, LANES, s), BF16),
        jax.ShapeDtypeStruct((b, A_HEADS, s, LANES), BF16),
        jax.ShapeDtypeStruct((b, A_HEADS, nt, V_ROWS, tm), BF16),
        jax.ShapeDtypeStruct((b, s // BLOCK, B_KV_HEADS, LANES, B_GROUP * BLOCK), BF16),
        jax.ShapeDtypeStruct((b, s, LANES), BF16),
        jax.ShapeDtypeStruct((b, s // BLOCK, B_KV_HEADS, V_ROWS, BLOCK), BF16),
    ]
    return pl.pallas_call(
        _proj_kernel, grid=grid, in_specs=in_specs, out_specs=out_specs, out_shape=out_shape,
        compiler_params=pltpu.CompilerParams(
            dimension_semantics=("parallel", "parallel"), vmem_limit_bytes=VMEM_LIMIT),
        name="mix_proj",
    )(x3, pre_g, win, wint, qn_g, wqt, kvn_g, wk, wvt, cq, sq, ck, sk)


def _flash_kernel(qt_ref, k_ref, vt_ref, o_ref, sa_ref, sb_ref, mca_ref, mcb_ref, m_ref, acc_ref,
                  *, nk):
    tk = FLASH_TK
    per = PROJ_TM // tk
    bufs = ((sa_ref, mca_ref), (sb_ref, mcb_ref))
    m_ref[...] = jnp.full(m_ref.shape, NEG_INF, F32)
    acc_ref[...] = jnp.zeros(acc_ref.shape, F32)

    tiles = [(j, slice(n * MXU_N, (n + 1) * MXU_N))
             for j in range(2) for n in range(qt_ref.shape[3] // MXU_N)]

    def score_tile(c, buf, j, ln):
        s_ref, mc_ref = buf
        off = pl.multiple_of(c * tk, tk)
        st = _dot(k_ref[0, j, pl.ds(off, tk), :], qt_ref[0, j, :, ln])
        s_ref[j, :, ln] = st
        mc_ref[j, :, ln] = jnp.max(st, axis=0, keepdims=True)

    def update_tile(vidx, vpart, buf, j, ln):
        s_ref, mc_ref = buf
        m_prev = m_ref[j, :, ln]
        m_new = jnp.maximum(m_prev, mc_ref[j, :, ln])
        alpha = jnp.exp2(m_prev - m_new)
        p = jnp.exp2(s_ref[j, :, ln] - m_new).astype(BF16)
        vt = vt_ref[0, j, vidx, :, vpart * tk:(vpart + 1) * tk]
        acc_ref[j, :, ln] = alpha * acc_ref[j, :, ln] + _dot(vt, p)
        m_ref[j, :, ln] = m_new

    def step(c_next, buf_next, vidx, vpart, buf_cur):
        score_tile(c_next, buf_next, *tiles[0])
        for i, t in enumerate(tiles):
            if i + 1 < len(tiles):
                score_tile(c_next, buf_next, *tiles[i + 1])
            update_tile(vidx, vpart, buf_cur, *t)

    for t in tiles:
        score_tile(0, bufs[0], *t)

    def body(i, carry):
        for h in range(FLASH_STEPS):
            step(FLASH_STEPS * i + h + 1, bufs[(h + 1) % 2],
                 (FLASH_STEPS // per) * i + h // per, h % per, bufs[h % 2])
        return carry

    n_loop = (nk - 1) // FLASH_STEPS
    lax.fori_loop(0, n_loop, body, 0)
    for c in range(n_loop * FLASH_STEPS, nk - 1):
        step(c + 1, bufs[(c + 1) % 2], c // per, c % per, bufs[c % 2])
    for t in tiles:
        update_tile((nk - 1) // per, (nk - 1) % per, bufs[(nk - 1) % 2], *t)
    outs = []
    for j in range(2):
        acc = acc_ref[j]
        outs.append(acc[:A_V] / acc[A_V:A_V + 1])
    o_ref[0] = jnp.concatenate(outs, axis=0).T.astype(BF16)


def _flash(qt, k, vt):
    b, _, _, s = qt.shape
    tq = FLASH_TQ
    nk = s // FLASH_TK
    grid = (b, A_HEADS // 2, s // tq)
    return pl.pallas_call(
        functools.partial(_flash_kernel, nk=nk),
        grid=grid,
        in_specs=[
            pl.BlockSpec((1, 2, LANES, tq), lambda i, h, j: (i, h, 0, j)),
            pl.BlockSpec((1, 2, s, LANES), lambda i, h, j: (i, h, 0, 0)),
            pl.BlockSpec((1, 2, s // PROJ_TM, V_ROWS, PROJ_TM), lambda i, h, j: (i, h, 0, 0, 0)),
        ],
        out_specs=pl.BlockSpec((1, tq, LANES), lambda i, h, j: (i, j, h)),
        out_shape=jax.ShapeDtypeStruct((b, s, A_HEADS * A_V), BF16),
        scratch_shapes=[pltpu.VMEM((2, FLASH_TK, tq), F32), pltpu.VMEM((2, FLASH_TK, tq), F32),
                        pltpu.VMEM((2, 1, tq), F32), pltpu.VMEM((2, 1, tq), F32),
                        pltpu.VMEM((2, 1, tq), F32), pltpu.VMEM((2, V_ROWS, tq), F32)],
        compiler_params=pltpu.CompilerParams(
            dimension_semantics=("parallel", "parallel", "arbitrary"),
            vmem_limit_bytes=VMEM_LIMIT),
        name="mla_flash",
    )(qt, k, vt)


def _bias_kernel(tab_ref, bkt_ref, o_ref):
    for v in range(3):
        bkt = bkt_ref[v]
        for hd in range(B_HEADS):
            acc = jnp.full(bkt.shape, NEG_INF, F32)
            for i in range(REL_BUCKETS):
                acc = jnp.where(bkt == i, tab_ref[i, hd] * LOG2E, acc)
            kvh, g = divmod(hd, B_GROUP)
            o_ref[v, kvh, :, g * BLOCK:(g + 1) * BLOCK] = acc


def _band_bias(rel_bias, buckets):
    return pl.pallas_call(
        _bias_kernel,
        in_specs=[pl.BlockSpec(memory_space=pltpu.SMEM),
                  pl.BlockSpec(memory_space=pltpu.VMEM)],
        out_specs=pl.BlockSpec(memory_space=pltpu.VMEM),
        out_shape=jax.ShapeDtypeStruct((3, B_KV_HEADS, 3 * BLOCK, B_GROUP * BLOCK), F32),
        name="band_bias",
    )(rel_bias, buckets)


def _swa_kernel(sink_ref, qt_ref, k_ref, vt_ref, bias_ref, o_ref, *, nblk):
    step = pl.program_id(1)
    per_step = SWA_TQ // BLOCK
    lane_head = lax.shift_right_logical(
        lax.broadcasted_iota(jnp.int32, (1, B_GROUP * BLOCK), 1), int(math.log2(BLOCK)))
    sinks = []
    for kvh in range(B_KV_HEADS):
        sv = jnp.zeros((1, B_GROUP * BLOCK), F32)
        for g in range(B_GROUP):
            sv = jnp.where(lane_head == g, sink_ref[kvh * B_GROUP + g] * LOG2E, sv)
        sinks.append(sv)

    def score(t, kvh):
        n = step * per_step + t
        nb0 = jnp.clip(n - 1, 0, nblk - 3)
        var = jnp.where(n == 0, 1, jnp.where(n == nblk - 1, 2, 0))
        kwin = k_ref[0, pl.ds(pl.multiple_of(nb0 * BLOCK, BLOCK), 3 * BLOCK), :]
        st = _dot(kwin, qt_ref[0, t, kvh]) + bias_ref[var, kvh]
        m = jnp.maximum(jnp.max(st, axis=0, keepdims=True), sinks[kvh])
        return st, m, nb0

    def attend(kvh, st, m, nb0):
        p = jnp.exp2(st - m).astype(BF16)
        acc = None
        for w in range(3):
            d = _dot(vt_ref[0, nb0 + w, kvh], p[w * BLOCK:(w + 1) * BLOCK])
            acc = d if acc is None else acc + d
        den = acc[B_HEAD_DIM:B_HEAD_DIM + 1] + jnp.exp2(sinks[kvh] - m)
        out = acc[:B_HEAD_DIM] / den
        return [out[:, g * BLOCK:(g + 1) * BLOCK] for g in range(B_GROUP)]

    def group(i, carry):
        chains = [(i * SWA_UNROLL + u, kvh) for u in range(SWA_UNROLL) for kvh in range(B_KV_HEADS)]
        scored = [score(*c) for c in chains[:SWA_LEAD]]
        rows = []
        for idx, (t, kvh) in enumerate(chains):
            if idx + SWA_LEAD < len(chains):
                scored.append(score(*chains[idx + SWA_LEAD]))
            rows += attend(kvh, *scored[idx])
            if kvh == B_KV_HEADS - 1:
                ot = jnp.concatenate(rows, axis=0)
                r0 = pl.multiple_of(t * BLOCK, BLOCK)
                o_ref[0, pl.ds(r0, BLOCK), :] = ot.T.astype(BF16)
                rows = []
        return carry

    lax.fori_loop(0, per_step // SWA_UNROLL, group, 0)


def _swa(sink, qst, ks, vst, biast):
    b, nblk = qst.shape[0], qst.shape[1]
    s = nblk * BLOCK
    per_step = SWA_TQ // BLOCK
    grid = (b, s // SWA_TQ)
    return pl.pallas_call(
        functools.partial(_swa_kernel, nblk=nblk),
        grid=grid,
        in_specs=[
            pl.BlockSpec(memory_space=pltpu.SMEM),
            pl.BlockSpec((1, per_step, B_KV_HEADS, LANES, B_GROUP * BLOCK),
                         lambda i, j: (i, j, 0, 0, 0)),
            pl.BlockSpec((1, s, LANES), lambda i, j: (i, 0, 0)),
            pl.BlockSpec((1, nblk, B_KV_HEADS, V_ROWS, BLOCK), lambda i, j: (i, 0, 0, 0, 0)),
            _const_spec((3, B_KV_HEADS, 3 * BLOCK, B_GROUP * BLOCK)),
        ],
        out_specs=pl.BlockSpec((1, SWA_TQ, SWA_COLS), lambda i, j: (i, j, 0)),
        out_shape=jax.ShapeDtypeStruct((b, s, SWA_COLS), BF16),
        compiler_params=pltpu.CompilerParams(
            dimension_semantics=("parallel", "arbitrary"), vmem_limit_bytes=VMEM_LIMIT),
        name="swa",
    )(sink, qst, ks, vst, biast)


def _t5_bucket(rel):
    nb = REL_BUCKETS // 2
    max_exact = nb // 2
    bucket = jnp.where(rel > 0, nb, 0)
    n = jnp.abs(rel)
    nf = jnp.maximum(n, 1).astype(jnp.float32)
    large = max_exact + (jnp.log(nf / max_exact) / math.log(REL_MAX_DIST / max_exact)
                         * (nb - max_exact)).astype(jnp.int32)
    large = jnp.minimum(large, nb - 1)
    return bucket + jnp.where(n < max_exact, n, large)


def _band_buckets():
    jj = jnp.arange(3 * BLOCK)[None, :, None]
    r = jnp.arange(BLOCK)[None, None, :]
    shift = jnp.array([0, BLOCK, -BLOCK])[:, None, None]
    j = jj + shift
    rel = j - BLOCK - r
    ok = (j >= 0) & (j < 3 * BLOCK) & (jnp.abs(rel) <= WINDOW)
    return jnp.where(ok, _t5_bucket(rel), -1).astype(jnp.int32)


def _rope_tables(seq):
    pos = jnp.arange(seq, dtype=jnp.float32)
    inv = ROPE_THETA ** (-jnp.arange(0, A_ROPE, 2, dtype=jnp.float32) / A_ROPE)
    ang = pos[:, None] * inv[None, :]
    cos, sin = jnp.cos(ang), jnp.sin(ang)
    cos2 = jnp.concatenate([cos, cos], axis=1)
    sin2 = jnp.concatenate([sin, sin], axis=1)
    z32 = jnp.zeros((seq, LANES - A_NOPE - A_ROPE), F32)
    ck = jnp.concatenate([jnp.zeros((seq, A_NOPE), F32), cos2, z32], axis=1)
    sk = jnp.concatenate([jnp.zeros((seq, A_NOPE), F32), sin2, z32], axis=1)
    scale = (A_NOPE + A_ROPE) ** -0.5 * LOG2E
    cq = jnp.concatenate([jnp.ones((seq, A_NOPE), F32), cos2, z32], axis=1) * scale
    return cq.T, (sk * scale).T, ck, sk


def _layout_tables():
    hr = A_ROPE // 2
    rot_src = np.concatenate([np.arange(hr, A_ROPE), np.arange(0, hr)])
    rot_sgn = np.concatenate([-np.ones(hr), np.ones(hr)])

    kr0 = A_Q_RANK + A_KV_RANK
    qb0 = kr0 + A_ROPE
    kb0 = qb0 + SWA_COLS
    vb0 = kb0 + B_KV_HEADS * B_HEAD_DIM

    src = np.zeros(N_IN, np.int32)
    sgn = np.zeros(N_IN, np.float32)
    src[_C_Q:_C_KR] = np.arange(kr0)
    sgn[_C_Q:_C_KR] = 1.0
    src[_C_KR + A_NOPE:_C_KR + A_NOPE + A_ROPE] = kr0 + np.arange(A_ROPE)
    sgn[_C_KR + A_NOPE:_C_KR + A_NOPE + A_ROPE] = 1.0
    src[_C_KRR + A_NOPE:_C_KRR + A_NOPE + A_ROPE] = kr0 + rot_src
    sgn[_C_KRR + A_NOPE:_C_KRR + A_NOPE + A_ROPE] = rot_sgn
    src[_C_KS:N_IN] = kb0 + np.arange(B_KV_HEADS * B_HEAD_DIM)
    sgn[_C_KS:N_IN] = 1.0

    t_src = np.zeros(N_INT, np.int32)
    t_sgn = np.zeros(N_INT, np.float32)
    t_src[:_R_VS] = qb0 + np.arange(SWA_COLS)
    t_sgn[:_R_VS] = 1.0
    for kvh in range(B_KV_HEADS):
        lo = _R_VS + kvh * V_ROWS
        t_src[lo:lo + B_HEAD_DIM] = vb0 + kvh * B_HEAD_DIM + np.arange(B_HEAD_DIM)
        t_sgn[lo:lo + B_HEAD_DIM] = 1.0

    qd = A_NOPE + A_ROPE
    assert qd + A_ROPE == LANES
    q_src = np.zeros(A_HEADS * LANES, np.int32)
    q_sgn = np.zeros(A_HEADS * LANES, np.float32)
    for hd in range(A_HEADS):
        lo = hd * LANES
        q_src[lo:lo + qd] = hd * qd + np.arange(qd)
        q_sgn[lo:lo + qd] = 1.0
        q_src[lo + qd:lo + LANES] = hd * qd + A_NOPE + rot_src
        q_sgn[lo + qd:lo + LANES] = rot_sgn

    kvd = A_NOPE + A_V
    k_src = np.zeros(A_HEADS * LANES, np.int32)
    k_sgn = np.zeros(A_HEADS * LANES, np.float32)
    v_src = np.zeros(A_HEADS * V_ROWS, np.int32)
    v_sgn = np.zeros(A_HEADS * V_ROWS, np.float32)
    for hd in range(A_HEADS):
        k_src[hd * LANES:hd * LANES + A_NOPE] = hd * kvd + np.arange(A_NOPE)
        k_sgn[hd * LANES:hd * LANES + A_NOPE] = 1.0
        v_src[hd * V_ROWS:hd * V_ROWS + A_V] = hd * kvd + A_NOPE + np.arange(A_V)
        v_sgn[hd * V_ROWS:hd * V_ROWS + A_V] = 1.0
    return (src, sgn), (t_src, t_sgn), (q_src, q_sgn), (k_src, k_sgn), (v_src, v_sgn)


def _gather_cols(w, src, sgn):
    return (jnp.take(w, jnp.asarray(src), axis=1) * jnp.asarray(sgn)[None, :]).astype(BF16)


def kernel(x, rel_bias, ffn1_pre_g, ffn1_w_gate, ffn1_w_up, ffn1_w_down, ffn1_post_g, mix_pre_g, w_in, mla_q_norm_g, mla_w_uq, mla_kv_norm_g, mla_w_ukv, swa_sink, w_out, mix_post_g, ffn2_pre_g, ffn2_w_gate, ffn2_w_up, ffn2_w_down, ffn2_post_g):
    b, s, d = x.shape
    depth = w_in.shape[0]
    assert d == D_MODEL and s % FLASH_TQ == 0 and s % SWA_TQ == 0 and s >= 3 * BLOCK
    assert s % PROJ_TM == 0 and PROJ_TM % FLASH_TK == 0
    assert FLASH_STEPS % 2 == 0 and FLASH_STEPS % (PROJ_TM // FLASH_TK) == 0
    assert (b * s) % FFN_TM == 0

    in_l, int_l, q_l, k_l, v_l = _layout_tables()
    cq, sq, ck, sk = _rope_tables(s)
    biast = _band_bias(rel_bias, _band_buckets())

    row = lambda g: g.reshape(1, -1)
    x2 = x.reshape(b * s, d)
    for i in range(depth):
        x2 = _ffn(x2, row(ffn1_pre_g[i]), ffn1_w_gate[i].astype(BF16), ffn1_w_up[i].astype(BF16),
                  ffn1_w_down[i].astype(BF16), row(ffn1_post_g[i]))
        win = _gather_cols(w_in[i], *in_l)
        wint = _gather_cols(w_in[i], *int_l).T
        wqt = _gather_cols(mla_w_uq[i], *q_l).T
        wk = _gather_cols(mla_w_ukv[i], *k_l)
        wvt = _gather_cols(mla_w_ukv[i], *v_l).T
        qt, k, vt, qst, ks, vst = _proj(x2.reshape(b, s, d), row(mix_pre_g[i]), win, wint,
                                        row(mla_q_norm_g[i]), wqt, row(mla_kv_norm_g[i]), wk, wvt,
                                        cq, sq, ck, sk)
        oa = _flash(qt, k, vt)
        ob = _swa(swa_sink[i], qst, ks, vst, biast)
        mixer = (oa.reshape(b * s, -1), ob.reshape(b * s, -1), w_out[i].astype(BF16),
                 row(mix_post_g[i]))
        x2 = _ffn(x2, row(ffn2_pre_g[i]), ffn2_w_gate[i].astype(BF16), ffn2_w_up[i].astype(BF16),
                  ffn2_w_down[i].astype(BF16), row(ffn2_post_g[i]), mixer=mixer)
    return x2.reshape(b, s, d)
```

```python
import functools
import math

import numpy as np
import jax
import jax.numpy as jnp
from jax import lax
from jax.experimental import pallas as pl
from jax.experimental.pallas import tpu as pltpu

D_MODEL = 1024
A_HEADS = 8
A_NOPE = 64
A_ROPE = 32
A_V = 64
A_Q_RANK = 384
A_KV_RANK = 256
B_HEADS = 8
B_KV_HEADS = 2
B_HEAD_DIM = 64
B_GROUP = B_HEADS // B_KV_HEADS
WINDOW = 128
BLOCK = 128
REL_BUCKETS = 32
REL_MAX_DIST = 128
D_FF = 2816
FFN_RES_WEIGHT = 0.5
ROPE_THETA = 10000.0
EPS = 1e-6
NEG_INF = -1e30
LOG2E = math.log2(math.e)

LANES = 128
MXU_N = 256
V_ROWS = 80
SWA_COLS = B_HEADS * B_HEAD_DIM
VMEM_LIMIT = 56 * 1024 * 1024

FFN_TM = 512
FFN_FC = 256
PROJ_TM = 512
FLASH_TQ = 2048
FLASH_TK = 256
FLASH_STEPS = 4
SWA_TQ = 1024
SWA_UNROLL = 4
SWA_LEAD = 2

_C_Q = 0
_C_KV = _C_Q + A_Q_RANK
_C_KR = _C_KV + A_KV_RANK
_C_KRR = _C_KR + LANES
_C_KS = _C_KRR + LANES
N_IN = _C_KS + B_KV_HEADS * B_HEAD_DIM
_R_VS = SWA_COLS
N_INT = _R_VS + B_KV_HEADS * V_ROWS

BF16 = jnp.bfloat16
F32 = jnp.float32


def _dot(a, b):
    return jnp.dot(a, b, preferred_element_type=F32)


def _dot_nt(a, b):
    return lax.dot_general(a, b, (((1,), (1,)), ((), ())), preferred_element_type=F32)


def _rms(x, g):
    return x * lax.rsqrt(jnp.mean(x * x, axis=-1, keepdims=True) + EPS) * g


def _const_spec(shape, layer=None):
    nd = len(shape)
    if layer is None:
        return pl.BlockSpec(shape, lambda *_: (0,) * nd, pipeline_mode=pl.Buffered(1))
    return pl.BlockSpec((None,) + tuple(shape), lambda *_: (layer,) + (0,) * nd,
                        pipeline_mode=pl.Buffered(1))


def _ffn_kernel(*refs, mixer):
    if mixer:
        oa_ref, ob_ref, wo_ref, mg_ref, x_ref, pre_ref, wg_ref, wu_ref, wd_ref, post_ref, o_ref = refs
        half = A_HEADS * A_V
        o = _dot(oa_ref[...], wo_ref[:half, :]) + _dot(ob_ref[...], wo_ref[half:, :])
        x = x_ref[...] + _rms(o, mg_ref[...])
    else:
        x_ref, pre_ref, wg_ref, wu_ref, wd_ref, post_ref, o_ref = refs
        x = x_ref[...]
    h = _rms(x, pre_ref[...]).astype(BF16)
    acc = None
    for c in range(0, D_FF, FFN_FC):
        g = _dot(h, wg_ref[:, c:c + FFN_FC])
        u = _dot(h, wu_ref[:, c:c + FFN_FC])
        a = (g * u / (1.0 + jnp.exp(-g))).astype(BF16)
        d = _dot(a, wd_ref[c:c + FFN_FC, :])
        acc = d if acc is None else acc + d
    o_ref[...] = x + FFN_RES_WEIGHT * _rms(acc, post_ref[...])


def _ffn(layer, x2, pre_g, wg, wu, wd, post_g, mixer=None):
    m = x2.shape[0]
    row = pl.BlockSpec((FFN_TM, D_MODEL), lambda i: (i, 0))
    in_specs = [row, _const_spec((1, D_MODEL), layer), _const_spec((D_MODEL, D_FF), layer),
                _const_spec((D_MODEL, D_FF), layer), _const_spec((D_FF, D_MODEL), layer),
                _const_spec((1, D_MODEL), layer)]
    args = (x2, pre_g, wg, wu, wd, post_g)
    if mixer is not None:
        half = pl.BlockSpec((FFN_TM, mixer[0].shape[1]), lambda i: (i, 0))
        in_specs = [half, half, _const_spec((D_MODEL, D_MODEL), layer),
                    _const_spec((1, D_MODEL), layer)] + in_specs
        args = tuple(mixer) + args
    return pl.pallas_call(
        functools.partial(_ffn_kernel, mixer=mixer is not None),
        grid=(m // FFN_TM,),
        in_specs=in_specs,
        out_specs=row,
        out_shape=jax.ShapeDtypeStruct((m, D_MODEL), F32),
        compiler_params=pltpu.CompilerParams(
            dimension_semantics=("parallel",), vmem_limit_bytes=VMEM_LIMIT),
        name="mix_ffn" if mixer is not None else "ffn",
    )(*args)


def _proj_kernel(x_ref, pre_ref, win_ref, wint_ref, qn_ref, wqt_ref, kvn_ref, wk_ref, wvt_ref,
                 cq_ref, sq_ref, ck_ref, sk_ref,
                 qt_ref, k_ref, vt_ref, qst_ref, ks_ref, vst_ref):
    h = _rms(x_ref[0], pre_ref[...]).astype(BF16)
    z = _dot(h, win_ref[...])
    zt = _dot_nt(wint_ref[...], h)
    cqn = _rms(z[:, _C_Q:_C_KV], qn_ref[...]).astype(BF16)
    ckvn = _rms(z[:, _C_KV:_C_KR], kvn_ref[...]).astype(BF16)

    qt2 = _dot_nt(wqt_ref[...], cqn)
    r0, r1 = A_NOPE, A_NOPE + A_ROPE
    cq = cq_ref[...]
    sq = sq_ref[...]
    pad = jnp.zeros((LANES - r1, x_ref.shape[1]), F32)
    for hd in range(A_HEADS):
        blk = qt2[hd * LANES:(hd + 1) * LANES]
        rope = blk[r0:r1] * cq[r0:r1] + blk[r1:] * sq[r0:r1]
        qt_ref[0, hd] = jnp.concatenate([blk[:r0] * cq[:r0], rope, pad], axis=0).astype(BF16)

    kr = z[:, _C_KR:_C_KRR] * ck_ref[...] + z[:, _C_KRR:_C_KS] * sk_ref[...]
    kn = _dot(ckvn, wk_ref[...])
    for hd in range(A_HEADS):
        lo = hd * LANES
        k_ref[0, hd] = (kn[:, lo:lo + LANES] + kr).astype(BF16)

    vt = _dot_nt(wvt_ref[...], ckvn)
    ones_row = (lax.broadcasted_iota(jnp.int32, (V_ROWS, 1), 0) == A_V).astype(F32)
    for hd in range(A_HEADS):
        lo = hd * V_ROWS
        vt_ref[0, hd, 0] = (vt[lo:lo + V_ROWS] + ones_row).astype(BF16)

    ks_ref[0] = z[:, _C_KS:N_IN].astype(BF16)
    qall = zt[:_R_VS] * (B_HEAD_DIM ** -0.5 * LOG2E)
    zero = jnp.zeros((B_HEAD_DIM, B_GROUP * BLOCK), F32)
    for blk in range(x_ref.shape[1] // BLOCK):
        cols = slice(blk * BLOCK, (blk + 1) * BLOCK)
        for kvh in range(B_KV_HEADS):
            piece = jnp.concatenate(
                [qall[(kvh * B_GROUP + g) * B_HEAD_DIM:(kvh * B_GROUP + g + 1) * B_HEAD_DIM, cols]
                 for g in range(B_GROUP)], axis=1)
            parts = [piece if i == kvh else zero for i in range(B_KV_HEADS)]
            qst_ref[0, blk, kvh] = jnp.concatenate(parts, axis=0).astype(BF16)
            lo = _R_VS + kvh * V_ROWS
            vst_ref[0, blk, kvh] = (zt[lo:lo + V_ROWS, cols] + ones_row).astype(BF16)


def _proj(layer, x3, pre_g, win, wint, qn_g, wqt, kvn_g, wk, wvt, cq, sq, ck, sk):
    b, s, _ = x3.shape
    tm = PROJ_TM
    nt = s // tm
    nb = tm // BLOCK
    grid = (b, nt)
    in_specs = [
        pl.BlockSpec((1, tm, D_MODEL), lambda i, j: (i, j, 0)),
        _const_spec((1, D_MODEL), layer), _const_spec((D_MODEL, N_IN), layer),
        _const_spec((N_INT, D_MODEL), layer),
        _const_spec((1, A_Q_RANK), layer), _const_spec((A_HEADS * LANES, A_Q_RANK), layer),
        _const_spec((1, A_KV_RANK), layer), _const_spec((A_KV_RANK, A_HEADS * LANES), layer),
        _const_spec((A_HEADS * V_ROWS, A_KV_RANK), layer),
        pl.BlockSpec((LANES, tm), lambda i, j: (0, j)),
        pl.BlockSpec((LANES, tm), lambda i, j: (0, j)),
        pl.BlockSpec((tm, LANES), lambda i, j: (j, 0)),
        pl.BlockSpec((tm, LANES), lambda i, j: (j, 0)),
    ]
    out_specs = [
        pl.BlockSpec((1, A_HEADS, LANES, tm), lambda i, j: (i, 0, 0, j)),
        pl.BlockSpec((1, A_HEADS, tm, LANES), lambda i, j: (i, 0, j, 0)),
        pl.BlockSpec((1, A_HEADS, 1, V_ROWS, tm), lambda i, j: (i, 0, j, 0, 0)),
        pl.BlockSpec((1, nb, B_KV_HEADS, LANES, B_GROUP * BLOCK), lambda i, j: (i, j, 0, 0, 0)),
        pl.BlockSpec((1, tm, LANES), lambda i, j: (i, j, 0)),
        pl.BlockSpec((1, nb, B_KV_HEADS, V_ROWS, BLOCK), lambda i, j: (i, j, 0, 0, 0)),
    ]
    out_shape = [
        jax.ShapeDtypeStruct((b, A_HEADS, LANES, s), BF16),
        jax.ShapeDtypeStruct((b, A_HEADS, s, LANES), BF16),
        jax.ShapeDtypeStruct((b, A_HEADS, nt, V_ROWS, tm), BF16),
        jax.ShapeDtypeStruct((b, s // BLOCK, B_KV_HEADS, LANES, B_GROUP * BLOCK), BF16),
        jax.ShapeDtypeStruct((b, s, LANES), BF16),
        jax.ShapeDtypeStruct((b, s // BLOCK, B_KV_HEADS, V_ROWS, BLOCK), BF16),
    ]
    return pl.pallas_call(
        _proj_kernel, grid=grid, in_specs=in_specs, out_specs=out_specs, out_shape=out_shape,
        compiler_params=pltpu.CompilerParams(
            dimension_semantics=("parallel", "parallel"), vmem_limit_bytes=VMEM_LIMIT),
        name="mix_proj",
    )(x3, pre_g, win, wint, qn_g, wqt, kvn_g, wk, wvt, cq, sq, ck, sk)


def _flash_kernel(qt_ref, k_ref, vt_ref, o_ref, sa_ref, sb_ref, mca_ref, mcb_ref, m_ref, acc_ref,
                  *, nk):
    tk = FLASH_TK
    per = PROJ_TM // tk
    bufs = ((sa_ref, mca_ref), (sb_ref, mcb_ref))
    m_ref[...] = jnp.full(m_ref.shape, NEG_INF, F32)
    acc_ref[...] = jnp.zeros(acc_ref.shape, F32)

    tiles = [(j, slice(n * MXU_N, (n + 1) * MXU_N))
             for j in range(2) for n in range(qt_ref.shape[3] // MXU_N)]

    def score_tile(c, buf, j, ln):
        s_ref, mc_ref = buf
        off = pl.multiple_of(c * tk, tk)
        st = _dot(k_ref[0, j, pl.ds(off, tk), :], qt_ref[0, j, :, ln])
        s_ref[j, :, ln] = st
        mc_ref[j, :, ln] = jnp.max(st, axis=0, keepdims=True)

    def update_tile(vidx, vpart, buf, j, ln):
        s_ref, mc_ref = buf
        m_prev = m_ref[j, :, ln]
        m_new = jnp.maximum(m_prev, mc_ref[j, :, ln])
        alpha = jnp.exp2(m_prev - m_new)
        p = jnp.exp2(s_ref[j, :, ln] - m_new).astype(BF16)
        vt = vt_ref[0, j, vidx, :, vpart * tk:(vpart + 1) * tk]
        acc_ref[j, :, ln] = alpha * acc_ref[j, :, ln] + _dot(vt, p)
        m_ref[j, :, ln] = m_new

    def step(c_next, buf_next, vidx, vpart, buf_cur):
        score_tile(c_next, buf_next, *tiles[0])
        for i, t in enumerate(tiles):
            if i + 1 < len(tiles):
                score_tile(c_next, buf_next, *tiles[i + 1])
            update_tile(vidx, vpart, buf_cur, *t)

    for t in tiles:
        score_tile(0, bufs[0], *t)

    def body(i, carry):
        for h in range(FLASH_STEPS):
            step(FLASH_STEPS * i + h + 1, bufs[(h + 1) % 2],
                 (FLASH_STEPS // per) * i + h // per, h % per, bufs[h % 2])
        return carry

    n_loop = (nk - 1) // FLASH_STEPS
    lax.fori_loop(0, n_loop, body, 0)
    for c in range(n_loop * FLASH_STEPS, nk - 1):
        step(c + 1, bufs[(c + 1) % 2], c // per, c % per, bufs[c % 2])
    for t in tiles:
        update_tile((nk - 1) // per, (nk - 1) % per, bufs[(nk - 1) % 2], *t)
    outs = []
    for j in range(2):
        acc = acc_ref[j]
        outs.append(acc[:A_V] / acc[A_V:A_V + 1])
    o_ref[0] = jnp.concatenate(outs, axis=0).T.astype(BF16)


def _flash(qt, k, vt):
    b, _, _, s = qt.shape
    tq = FLASH_TQ
    nk = s // FLASH_TK
    grid = (b, A_HEADS // 2, s // tq)
    return pl.pallas_call(
        functools.partial(_flash_kernel, nk=nk),
        grid=grid,
        in_specs=[
            pl.BlockSpec((1, 2, LANES, tq), lambda i, h, j: (i, h, 0, j)),
            pl.BlockSpec((1, 2, s, LANES), lambda i, h, j: (i, h, 0, 0)),
            pl.BlockSpec((1, 2, s // PROJ_TM, V_ROWS, PROJ_TM), lambda i, h, j: (i, h, 0, 0, 0)),
        ],
        out_specs=pl.BlockSpec((1, tq, LANES), lambda i, h, j: (i, j, h)),
        out_shape=jax.ShapeDtypeStruct((b, s, A_HEADS * A_V), BF16),
        scratch_shapes=[pltpu.VMEM((2, FLASH_TK, tq), F32), pltpu.VMEM((2, FLASH_TK, tq), F32),
                        pltpu.VMEM((2, 1, tq), F32), pltpu.VMEM((2, 1, tq), F32),
                        pltpu.VMEM((2, 1, tq), F32), pltpu.VMEM((2, V_ROWS, tq), F32)],
        compiler_params=pltpu.CompilerParams(
            dimension_semantics=("parallel", "parallel", "arbitrary"),
            vmem_limit_bytes=VMEM_LIMIT),
        name="mla_flash",
    )(qt, k, vt)


def _bias_kernel(tab_ref, bkt_ref, o_ref):
    for v in range(3):
        bkt = bkt_ref[v]
        for hd in range(B_HEADS):
            acc = jnp.full(bkt.shape, NEG_INF, F32)
            for i in range(REL_BUCKETS):
                acc = jnp.where(bkt == i, tab_ref[i, hd] * LOG2E, acc)
            kvh, g = divmod(hd, B_GROUP)
            o_ref[v, kvh, :, g * BLOCK:(g + 1) * BLOCK] = acc


def _band_bias(rel_bias, buckets):
    return pl.pallas_call(
        _bias_kernel,
        in_specs=[pl.BlockSpec(memory_space=pltpu.SMEM),
                  pl.BlockSpec(memory_space=pltpu.VMEM)],
        out_specs=pl.BlockSpec(memory_space=pltpu.VMEM),
        out_shape=jax.ShapeDtypeStruct((3, B_KV_HEADS, 3 * BLOCK, B_GROUP * BLOCK), F32),
        name="band_bias",
    )(rel_bias, buckets)


def _swa_kernel(sink_ref, qt_ref, k_ref, vt_ref, bias_ref, o_ref, *, nblk):
    step = pl.program_id(1)
    per_step = SWA_TQ // BLOCK
    lane_head = lax.shift_right_logical(
        lax.broadcasted_iota(jnp.int32, (1, B_GROUP * BLOCK), 1), int(math.log2(BLOCK)))
    sinks = []
    for kvh in range(B_KV_HEADS):
        sv = jnp.zeros((1, B_GROUP * BLOCK), F32)
        for g in range(B_GROUP):
            sv = jnp.where(lane_head == g, sink_ref[kvh * B_GROUP + g] * LOG2E, sv)
        sinks.append(sv)

    def score(t, kvh):
        n = step * per_step + t
        nb0 = jnp.clip(n - 1, 0, nblk - 3)
        var = jnp.where(n == 0, 1, jnp.where(n == nblk - 1, 2, 0))
        kwin = k_ref[0, pl.ds(pl.multiple_of(nb0 * BLOCK, BLOCK), 3 * BLOCK), :]
        st = _dot(kwin, qt_ref[0, t, kvh]) + bias_ref[var, kvh]
        m = jnp.maximum(jnp.max(st, axis=0, keepdims=True), sinks[kvh])
        return st, m, nb0

    def attend(kvh, st, m, nb0):
        p = jnp.exp2(st - m).astype(BF16)
        acc = None
        for w in range(3):
            d = _dot(vt_ref[0, nb0 + w, kvh], p[w * BLOCK:(w + 1) * BLOCK])
            acc = d if acc is None else acc + d
        den = acc[B_HEAD_DIM:B_HEAD_DIM + 1] + jnp.exp2(sinks[kvh] - m)
        out = acc[:B_HEAD_DIM] / den
        return [out[:, g * BLOCK:(g + 1) * BLOCK] for g in range(B_GROUP)]

    def group(i, carry):
        chains = [(i * SWA_UNROLL + u, kvh) for u in range(SWA_UNROLL) for kvh in range(B_KV_HEADS)]
        scored = [score(*c) for c in chains[:SWA_LEAD]]
        rows = []
        for idx, (t, kvh) in enumerate(chains):
            if idx + SWA_LEAD < len(chains):
                scored.append(score(*chains[idx + SWA_LEAD]))
            rows += attend(kvh, *scored[idx])
            if kvh == B_KV_HEADS - 1:
                ot = jnp.concatenate(rows, axis=0)
                r0 = pl.multiple_of(t * BLOCK, BLOCK)
                o_ref[0, pl.ds(r0, BLOCK), :] = ot.T.astype(BF16)
                rows = []
        return carry

    lax.fori_loop(0, per_step // SWA_UNROLL, group, 0)


def _swa(sink, qst, ks, vst, biast):
    b, nblk = qst.shape[0], qst.shape[1]
    s = nblk * BLOCK
    per_step = SWA_TQ // BLOCK
    grid = (b, s // SWA_TQ)
    return pl.pallas_call(
        functools.partial(_swa_kernel, nblk=nblk),
        grid=grid,
        in_specs=[
            pl.BlockSpec(memory_space=pltpu.SMEM),
            pl.BlockSpec((1, per_step, B_KV_HEADS, LANES, B_GROUP * BLOCK),
                         lambda i, j: (i, j, 0, 0, 0)),
            pl.BlockSpec((1, s, LANES), lambda i, j: (i, 0, 0)),
            pl.BlockSpec((1, nblk, B_KV_HEADS, V_ROWS, BLOCK), lambda i, j: (i, 0, 0, 0, 0)),
            _const_spec((3, B_KV_HEADS, 3 * BLOCK, B_GROUP * BLOCK)),
        ],
        out_specs=pl.BlockSpec((1, SWA_TQ, SWA_COLS), lambda i, j: (i, j, 0)),
        out_shape=jax.ShapeDtypeStruct((b, s, SWA_COLS), BF16),
        compiler_params=pltpu.CompilerParams(
            dimension_semantics=("parallel", "arbitrary"), vmem_limit_bytes=VMEM_LIMIT),
        name="swa",
    )(sink, qst, ks, vst, biast)


def _t5_bucket(rel):
    nb = REL_BUCKETS // 2
    max_exact = nb // 2
    bucket = jnp.where(rel > 0, nb, 0)
    n = jnp.abs(rel)
    nf = jnp.maximum(n, 1).astype(jnp.float32)
    large = max_exact + (jnp.log(nf / max_exact) / math.log(REL_MAX_DIST / max_exact)
                         * (nb - max_exact)).astype(jnp.int32)
    large = jnp.minimum(large, nb - 1)
    return bucket + jnp.where(n < max_exact, n, large)


def _band_buckets():
    jj = jnp.arange(3 * BLOCK)[None, :, None]
    r = jnp.arange(BLOCK)[None, None, :]
    shift = jnp.array([0, BLOCK, -BLOCK])[:, None, None]
    j = jj + shift
    rel = j - BLOCK - r
    ok = (j >= 0) & (j < 3 * BLOCK) & (jnp.abs(rel) <= WINDOW)
    return jnp.where(ok, _t5_bucket(rel), -1).astype(jnp.int32)


def _rope_tables(seq):
    pos = jnp.arange(seq, dtype=jnp.float32)
    inv = ROPE_THETA ** (-jnp.arange(0, A_ROPE, 2, dtype=jnp.float32) / A_ROPE)
    ang = pos[:, None] * inv[None, :]
    cos, sin = jnp.cos(ang), jnp.sin(ang)
    cos2 = jnp.concatenate([cos, cos], axis=1)
    sin2 = jnp.concatenate([sin, sin], axis=1)
    z32 = jnp.zeros((seq, LANES - A_NOPE - A_ROPE), F32)
    ck = jnp.concatenate([jnp.zeros((seq, A_NOPE), F32), cos2, z32], axis=1)
    sk = jnp.concatenate([jnp.zeros((seq, A_NOPE), F32), sin2, z32], axis=1)
    scale = (A_NOPE + A_ROPE) ** -0.5 * LOG2E
    cq = jnp.concatenate([jnp.ones((seq, A_NOPE), F32), cos2, z32], axis=1) * scale
    return cq.T, (sk * scale).T, ck, sk


def _layout_tables():
    hr = A_ROPE // 2
    rot_src = np.concatenate([np.arange(hr, A_ROPE), np.arange(0, hr)])
    rot_sgn = np.concatenate([-np.ones(hr), np.ones(hr)])

    kr0 = A_Q_RANK + A_KV_RANK
    qb0 = kr0 + A_ROPE
    kb0 = qb0 + SWA_COLS
    vb0 = kb0 + B_KV_HEADS * B_HEAD_DIM

    src = np.zeros(N_IN, np.int32)
    sgn = np.zeros(N_IN, np.float32)
    src[_C_Q:_C_KR] = np.arange(kr0)
    sgn[_C_Q:_C_KR] = 1.0
    src[_C_KR + A_NOPE:_C_KR + A_NOPE + A_ROPE] = kr0 + np.arange(A_ROPE)
    sgn[_C_KR + A_NOPE:_C_KR + A_NOPE + A_ROPE] = 1.0
    src[_C_KRR + A_NOPE:_C_KRR + A_NOPE + A_ROPE] = kr0 + rot_src
    sgn[_C_KRR + A_NOPE:_C_KRR + A_NOPE + A_ROPE] = rot_sgn
    src[_C_KS:N_IN] = kb0 + np.arange(B_KV_HEADS * B_HEAD_DIM)
    sgn[_C_KS:N_IN] = 1.0

    t_src = np.zeros(N_INT, np.int32)
    t_sgn = np.zeros(N_INT, np.float32)
    t_src[:_R_VS] = qb0 + np.arange(SWA_COLS)
    t_sgn[:_R_VS] = 1.0
    for kvh in range(B_KV_HEADS):
        lo = _R_VS + kvh * V_ROWS
        t_src[lo:lo + B_HEAD_DIM] = vb0 + kvh * B_HEAD_DIM + np.arange(B_HEAD_DIM)
        t_sgn[lo:lo + B_HEAD_DIM] = 1.0

    qd = A_NOPE + A_ROPE
    assert qd + A_ROPE == LANES
    q_src = np.zeros(A_HEADS * LANES, np.int32)
    q_sgn = np.zeros(A_HEADS * LANES, np.float32)
    for hd in range(A_HEADS):
        lo = hd * LANES
        q_src[lo:lo + qd] = hd * qd + np.arange(qd)
        q_sgn[lo:lo + qd] = 1.0
        q_src[lo + qd:lo + LANES] = hd * qd + A_NOPE + rot_src
        q_sgn[lo + qd:lo + LANES] = rot_sgn

    kvd = A_NOPE + A_V
    k_src = np.zeros(A_HEADS * LANES, np.int32)
    k_sgn = np.zeros(A_HEADS * LANES, np.float32)
    v_src = np.zeros(A_HEADS * V_ROWS, np.int32)
    v_sgn = np.zeros(A_HEADS * V_ROWS, np.float32)
    for hd in range(A_HEADS):
        k_src[hd * LANES:hd * LANES + A_NOPE] = hd * kvd + np.arange(A_NOPE)
        k_sgn[hd * LANES:hd * LANES + A_NOPE] = 1.0
        v_src[hd * V_ROWS:hd * V_ROWS + A_V] = hd * kvd + A_NOPE + np.arange(A_V)
        v_sgn[hd * V_ROWS:hd * V_ROWS + A_V] = 1.0
    return (src, sgn), (t_src, t_sgn), (q_src, q_sgn), (k_src, k_sgn), (v_src, v_sgn)


def _gather_cols(w, src, sgn):
    parts, c, n = [], 0, len(src)
    while c < n:
        e = c + 1
        while e < n and sgn[e] == sgn[c] and (sgn[c] == 0 or src[e] == src[e - 1] + 1):
            e += 1
        if sgn[c] == 0:
            parts.append(jnp.zeros(w.shape[:-1] + (e - c,), w.dtype))
        else:
            piece = w[..., int(src[c]):int(src[c]) + e - c]
            parts.append(piece if sgn[c] > 0 else -piece)
        c = e
    return jnp.concatenate(parts, axis=-1).astype(BF16)


def kernel(x, rel_bias, ffn1_pre_g, ffn1_w_gate, ffn1_w_up, ffn1_w_down, ffn1_post_g, mix_pre_g, w_in, mla_q_norm_g, mla_w_uq, mla_kv_norm_g, mla_w_ukv, swa_sink, w_out, mix_post_g, ffn2_pre_g, ffn2_w_gate, ffn2_w_up, ffn2_w_down, ffn2_post_g):
    b, s, d = x.shape
    depth = w_in.shape[0]
    assert d == D_MODEL and s % FLASH_TQ == 0 and s % SWA_TQ == 0 and s >= 3 * BLOCK
    assert s % PROJ_TM == 0 and PROJ_TM % FLASH_TK == 0
    assert FLASH_STEPS % 2 == 0 and FLASH_STEPS % (PROJ_TM // FLASH_TK) == 0
    assert (b * s) % FFN_TM == 0

    in_l, int_l, q_l, k_l, v_l = _layout_tables()
    cq, sq, ck, sk = _rope_tables(s)
    biast = _band_bias(rel_bias, _band_buckets())

    rows = lambda g: g.reshape(depth, 1, -1)
    bf = lambda w: w.astype(BF16)
    ffn1 = (rows(ffn1_pre_g), bf(ffn1_w_gate), bf(ffn1_w_up), bf(ffn1_w_down), rows(ffn1_post_g))
    ffn2 = (rows(ffn2_pre_g), bf(ffn2_w_gate), bf(ffn2_w_up), bf(ffn2_w_down), rows(ffn2_post_g))
    proj = (rows(mix_pre_g), _gather_cols(w_in, *in_l), _gather_cols(w_in, *int_l).swapaxes(1, 2),
            rows(mla_q_norm_g), _gather_cols(mla_w_uq, *q_l).swapaxes(1, 2),
            rows(mla_kv_norm_g), _gather_cols(mla_w_ukv, *k_l),
            _gather_cols(mla_w_ukv, *v_l).swapaxes(1, 2))
    wo, mix_g = bf(w_out), rows(mix_post_g)

    x2 = x.reshape(b * s, d)
    for i in range(depth):
        x2 = _ffn(i, x2, *ffn1)
        qt, k, vt, qst, ks, vst = _proj(i, x2.reshape(b, s, d), *proj, cq, sq, ck, sk)
        oa = _flash(qt, k, vt)
        ob = _swa(swa_sink[i], qst, ks, vst, biast)
        x2 = _ffn(i, x2, *ffn2, mixer=(oa.reshape(b * s, -1), ob.reshape(b * s, -1), wo, mix_g))
    return x2.reshape(b, s, d)
```

```python
import functools
import math

import numpy as np
import jax
import jax.numpy as jnp
from jax import lax
from jax.experimental import pallas as pl
from jax.experimental.pallas import tpu as pltpu

D_MODEL = 1024
A_HEADS = 8
A_NOPE = 64
A_ROPE = 32
A_V = 64
A_Q_RANK = 384
A_KV_RANK = 256
B_HEADS = 8
B_KV_HEADS = 2
B_HEAD_DIM = 64
B_GROUP = B_HEADS // B_KV_HEADS
WINDOW = 128
BLOCK = 128
REL_BUCKETS = 32
REL_MAX_DIST = 128
D_FF = 2816
FFN_RES_WEIGHT = 0.5
ROPE_THETA = 10000.0
EPS = 1e-6
NEG_INF = -1e30
LOG2E = math.log2(math.e)

LANES = 128
MXU_N = 256
V_ROWS = 80
SWA_COLS = B_HEADS * B_HEAD_DIM
VMEM_LIMIT = 56 * 1024 * 1024

FFN_TM = 512
FFN_FC = 256
PROJ_TM = 512
FLASH_TQ = 4096
FLASH_TK = 256
FLASH_STEPS = 4
SWA_TQ = 1024
SWA_UNROLL = 8
SWA_LEAD = 2

_C_Q = 0
_C_KV = _C_Q + A_Q_RANK
_C_KR = _C_KV + A_KV_RANK
_C_KRR = _C_KR + LANES
_C_KS = _C_KRR + LANES
N_IN = _C_KS + B_KV_HEADS * B_HEAD_DIM
_R_VS = SWA_COLS
N_INT = _R_VS + B_KV_HEADS * V_ROWS

BF16 = jnp.bfloat16
F32 = jnp.float32


def _dot(a, b):
    return jnp.dot(a, b, preferred_element_type=F32)


def _dot_nt(a, b):
    return lax.dot_general(a, b, (((1,), (1,)), ((), ())), preferred_element_type=F32)


def _rms(x, g):
    return x * lax.rsqrt(jnp.mean(x * x, axis=-1, keepdims=True) + EPS) * g


def _const_spec(shape, layer=None):
    nd = len(shape)
    if layer is None:
        return pl.BlockSpec(shape, lambda *_: (0,) * nd, pipeline_mode=pl.Buffered(1))
    return pl.BlockSpec((None,) + tuple(shape), lambda *_: (layer,) + (0,) * nd,
                        pipeline_mode=pl.Buffered(1))


def _ffn_kernel(*refs, mixer):
    if mixer:
        oa_ref, ob_ref, wo_ref, mg_ref, x_ref, pre_ref, wg_ref, wu_ref, wd_ref, post_ref, o_ref = refs
        half = A_HEADS * A_V
        o = _dot(oa_ref[...], wo_ref[:half, :]) + _dot(ob_ref[...], wo_ref[half:, :])
        x = x_ref[...] + _rms(o, mg_ref[...])
    else:
        x_ref, pre_ref, wg_ref, wu_ref, wd_ref, post_ref, o_ref = refs
        x = x_ref[...]
    h = _rms(x, pre_ref[...]).astype(BF16)
    acc = None
    for c in range(0, D_FF, FFN_FC):
        g = _dot(h, wg_ref[:, c:c + FFN_FC])
        u = _dot(h, wu_ref[:, c:c + FFN_FC])
        a = (g * u / (1.0 + jnp.exp(-g))).astype(BF16)
        d = _dot(a, wd_ref[c:c + FFN_FC, :])
        acc = d if acc is None else acc + d
    o_ref[...] = x + FFN_RES_WEIGHT * _rms(acc, post_ref[...])


def _ffn(layer, x2, pre_g, wg, wu, wd, post_g, mixer=None):
    m = x2.shape[0]
    row = pl.BlockSpec((FFN_TM, D_MODEL), lambda i: (i, 0))
    in_specs = [row, _const_spec((1, D_MODEL), layer), _const_spec((D_MODEL, D_FF), layer),
                _const_spec((D_MODEL, D_FF), layer), _const_spec((D_FF, D_MODEL), layer),
                _const_spec((1, D_MODEL), layer)]
    args = (x2, pre_g, wg, wu, wd, post_g)
    if mixer is not None:
        half = pl.BlockSpec((FFN_TM, mixer[0].shape[1]), lambda i: (i, 0))
        in_specs = [half, half, _const_spec((D_MODEL, D_MODEL), layer),
                    _const_spec((1, D_MODEL), layer)] + in_specs
        args = tuple(mixer) + args
    return pl.pallas_call(
        functools.partial(_ffn_kernel, mixer=mixer is not None),
        grid=(m // FFN_TM,),
        in_specs=in_specs,
        out_specs=row,
        out_shape=jax.ShapeDtypeStruct((m, D_MODEL), F32),
        compiler_params=pltpu.CompilerParams(
            dimension_semantics=("parallel",), vmem_limit_bytes=VMEM_LIMIT),
        name="mix_ffn" if mixer is not None else "ffn",
    )(*args)


def _proj_kernel(x_ref, pre_ref, win_ref, wint_ref, qn_ref, wqt_ref, kvn_ref, wk_ref, wvt_ref,
                 cq_ref, sq_ref, ck_ref, sk_ref,
                 qt_ref, k_ref, vt_ref, qst_ref, ks_ref, vst_ref):
    h = _rms(x_ref[0], pre_ref[...]).astype(BF16)
    z = _dot(h, win_ref[...])
    zt = _dot_nt(wint_ref[...], h)
    cqn = _rms(z[:, _C_Q:_C_KV], qn_ref[...]).astype(BF16)
    ckvn = _rms(z[:, _C_KV:_C_KR], kvn_ref[...]).astype(BF16)

    qt2 = _dot_nt(wqt_ref[...], cqn)
    r0, r1 = A_NOPE, A_NOPE + A_ROPE
    cq = cq_ref[...]
    sq = sq_ref[...]
    pad = jnp.zeros((LANES - r1, x_ref.shape[1]), F32)
    for hd in range(A_HEADS):
        blk = qt2[hd * LANES:(hd + 1) * LANES]
        rope = blk[r0:r1] * cq[r0:r1] + blk[r1:] * sq[r0:r1]
        qt_ref[0, hd] = jnp.concatenate([blk[:r0] * cq[:r0], rope, pad], axis=0).astype(BF16)

    kr = z[:, _C_KR:_C_KRR] * ck_ref[...] + z[:, _C_KRR:_C_KS] * sk_ref[...]
    kn = _dot(ckvn, wk_ref[...])
    for hd in range(A_HEADS):
        lo = hd * LANES
        k_ref[0, hd] = (kn[:, lo:lo + LANES] + kr).astype(BF16)

    vt = _dot_nt(wvt_ref[...], ckvn)
    ones_row = (lax.broadcasted_iota(jnp.int32, (V_ROWS, 1), 0) == A_V).astype(F32)
    for hd in range(A_HEADS):
        lo = hd * V_ROWS
        vt_ref[0, hd, 0] = (vt[lo:lo + V_ROWS] + ones_row).astype(BF16)

    ks_ref[0] = z[:, _C_KS:N_IN].astype(BF16)
    qall = zt[:_R_VS] * (B_HEAD_DIM ** -0.5 * LOG2E)
    zero = jnp.zeros((B_HEAD_DIM, B_GROUP * BLOCK), F32)
    for blk in range(x_ref.shape[1] // BLOCK):
        cols = slice(blk * BLOCK, (blk + 1) * BLOCK)
        for kvh in range(B_KV_HEADS):
            piece = jnp.concatenate(
                [qall[(kvh * B_GROUP + g) * B_HEAD_DIM:(kvh * B_GROUP + g + 1) * B_HEAD_DIM, cols]
                 for g in range(B_GROUP)], axis=1)
            parts = [piece if i == kvh else zero for i in range(B_KV_HEADS)]
            qst_ref[0, blk, kvh] = jnp.concatenate(parts, axis=0).astype(BF16)
            lo = _R_VS + kvh * V_ROWS
            vst_ref[0, blk, kvh] = (zt[lo:lo + V_ROWS, cols] + ones_row).astype(BF16)


def _proj(layer, x3, pre_g, win, wint, qn_g, wqt, kvn_g, wk, wvt, cq, sq, ck, sk):
    b, s, _ = x3.shape
    tm = PROJ_TM
    nt = s // tm
    nb = tm // BLOCK
    grid = (b, nt)
    in_specs = [
        pl.BlockSpec((1, tm, D_MODEL), lambda i, j: (i, j, 0)),
        _const_spec((1, D_MODEL), layer), _const_spec((D_MODEL, N_IN), layer),
        _const_spec((N_INT, D_MODEL), layer),
        _const_spec((1, A_Q_RANK), layer), _const_spec((A_HEADS * LANES, A_Q_RANK), layer),
        _const_spec((1, A_KV_RANK), layer), _const_spec((A_KV_RANK, A_HEADS * LANES), layer),
        _const_spec((A_HEADS * V_ROWS, A_KV_RANK), layer),
        pl.BlockSpec((LANES, tm), lambda i, j: (0, j)),
        pl.BlockSpec((LANES, tm), lambda i, j: (0, j)),
        pl.BlockSpec((tm, LANES), lambda i, j: (j, 0)),
        pl.BlockSpec((tm, LANES), lambda i, j: (j, 0)),
    ]
    out_specs = [
        pl.BlockSpec((1, A_HEADS, LANES, tm), lambda i, j: (i, 0, 0, j)),
        pl.BlockSpec((1, A_HEADS, tm, LANES), lambda i, j: (i, 0, j, 0)),
        pl.BlockSpec((1, A_HEADS, 1, V_ROWS, tm), lambda i, j: (i, 0, j, 0, 0)),
        pl.BlockSpec((1, nb, B_KV_HEADS, LANES, B_GROUP * BLOCK), lambda i, j: (i, j, 0, 0, 0)),
        pl.BlockSpec((1, tm, LANES), lambda i, j: (i, j, 0)),
        pl.BlockSpec((1, nb, B_KV_HEADS, V_ROWS, BLOCK), lambda i, j: (i, j, 0, 0, 0)),
    ]
    out_shape = [
        jax.ShapeDtypeStruct((b, A_HEADS, LANES, s), BF16),
        jax.ShapeDtypeStruct((b, A_HEADS, s, LANES), BF16),
        jax.ShapeDtypeStruct((b, A_HEADS, nt, V_ROWS, tm), BF16),
        jax.ShapeDtypeStruct((b, s // BLOCK, B_KV_HEADS, LANES, B_GROUP * BLOCK), BF16),
        jax.ShapeDtypeStruct((b, s, LANES), BF16),
        jax.ShapeDtypeStruct((b, s // BLOCK, B_KV_HEADS, V_ROWS, BLOCK), BF16),
    ]
    return pl.pallas_call(
        _proj_kernel, grid=grid, in_specs=in_specs, out_specs=out_specs, out_shape=out_shape,
        compiler_params=pltpu.CompilerParams(
            dimension_semantics=("parallel", "parallel"), vmem_limit_bytes=VMEM_LIMIT),
        name="mix_proj",
    )(x3, pre_g, win, wint, qn_g, wqt, kvn_g, wk, wvt, cq, sq, ck, sk)


def _flash_kernel(qt_ref, k_ref, vt_ref, o_ref, sa_ref, sb_ref, mca_ref, mcb_ref, m_ref, acc_ref,
                  *, nk):
    tk = FLASH_TK
    per = PROJ_TM // tk
    bufs = ((sa_ref, mca_ref), (sb_ref, mcb_ref))
    m_ref[...] = jnp.full(m_ref.shape, NEG_INF, F32)
    acc_ref[...] = jnp.zeros(acc_ref.shape, F32)

    tiles = [(j, slice(n * MXU_N, (n + 1) * MXU_N))
             for j in range(2) for n in range(qt_ref.shape[3] // MXU_N)]

    def score_tile(c, buf, j, ln):
        s_ref, mc_ref = buf
        off = pl.multiple_of(c * tk, tk)
        st = _dot(k_ref[0, j, pl.ds(off, tk), :], qt_ref[0, j, :, ln])
        s_ref[j, :, ln] = st
        mc_ref[j, :, ln] = jnp.max(st, axis=0, keepdims=True)

    def update_tile(vidx, vpart, buf, j, ln):
        s_ref, mc_ref = buf
        m_prev = m_ref[j, :, ln]
        m_new = jnp.maximum(m_prev, mc_ref[j, :, ln])
        alpha = jnp.exp2(m_prev - m_new)
        p = jnp.exp2(s_ref[j, :, ln] - m_new).astype(BF16)
        vt = vt_ref[0, j, vidx, :, vpart * tk:(vpart + 1) * tk]
        acc_ref[j, :, ln] = alpha * acc_ref[j, :, ln] + _dot(vt, p)
        m_ref[j, :, ln] = m_new

    def step(c_next, buf_next, vidx, vpart, buf_cur):
        score_tile(c_next, buf_next, *tiles[0])
        for i, t in enumerate(tiles):
            if i + 1 < len(tiles):
                score_tile(c_next, buf_next, *tiles[i + 1])
            update_tile(vidx, vpart, buf_cur, *t)

    for t in tiles:
        score_tile(0, bufs[0], *t)

    def body(i, carry):
        for h in range(FLASH_STEPS):
            step(FLASH_STEPS * i + h + 1, bufs[(h + 1) % 2],
                 (FLASH_STEPS // per) * i + h // per, h % per, bufs[h % 2])
        return carry

    n_loop = (nk - 1) // FLASH_STEPS
    lax.fori_loop(0, n_loop, body, 0)
    for c in range(n_loop * FLASH_STEPS, nk - 1):
        step(c + 1, bufs[(c + 1) % 2], c // per, c % per, bufs[c % 2])
    for t in tiles:
        update_tile((nk - 1) // per, (nk - 1) % per, bufs[(nk - 1) % 2], *t)
    outs = []
    for j in range(2):
        acc = acc_ref[j]
        outs.append(acc[:A_V] / acc[A_V:A_V + 1])
    o_ref[0] = jnp.concatenate(outs, axis=0).T.astype(BF16)


def _flash(qt, k, vt):
    b, _, _, s = qt.shape
    tq = FLASH_TQ
    nk = s // FLASH_TK
    grid = (b, A_HEADS // 2, s // tq)
    return pl.pallas_call(
        functools.partial(_flash_kernel, nk=nk),
        grid=grid,
        in_specs=[
            pl.BlockSpec((1, 2, LANES, tq), lambda i, h, j: (i, h, 0, j)),
            pl.BlockSpec((1, 2, s, LANES), lambda i, h, j: (i, h, 0, 0)),
            pl.BlockSpec((1, 2, s // PROJ_TM, V_ROWS, PROJ_TM), lambda i, h, j: (i, h, 0, 0, 0)),
        ],
        out_specs=pl.BlockSpec((1, tq, LANES), lambda i, h, j: (i, j, h)),
        out_shape=jax.ShapeDtypeStruct((b, s, A_HEADS * A_V), BF16),
        scratch_shapes=[pltpu.VMEM((2, FLASH_TK, tq), F32), pltpu.VMEM((2, FLASH_TK, tq), F32),
                        pltpu.VMEM((2, 1, tq), F32), pltpu.VMEM((2, 1, tq), F32),
                        pltpu.VMEM((2, 1, tq), F32), pltpu.VMEM((2, V_ROWS, tq), F32)],
        compiler_params=pltpu.CompilerParams(
            dimension_semantics=("parallel", "parallel", "arbitrary"),
            vmem_limit_bytes=VMEM_LIMIT),
        name="mla_flash",
    )(qt, k, vt)


def _bias_kernel(tab_ref, bkt_ref, o_ref):
    for v in range(3):
        bkt = bkt_ref[v]
        for hd in range(B_HEADS):
            acc = jnp.full(bkt.shape, NEG_INF, F32)
            for i in range(REL_BUCKETS):
                acc = jnp.where(bkt == i, tab_ref[i, hd] * LOG2E, acc)
            kvh, g = divmod(hd, B_GROUP)
            o_ref[v, kvh, :, g * BLOCK:(g + 1) * BLOCK] = acc


def _band_bias(rel_bias, buckets):
    return pl.pallas_call(
        _bias_kernel,
        in_specs=[pl.BlockSpec(memory_space=pltpu.SMEM),
                  pl.BlockSpec(memory_space=pltpu.VMEM)],
        out_specs=pl.BlockSpec(memory_space=pltpu.VMEM),
        out_shape=jax.ShapeDtypeStruct((3, B_KV_HEADS, 3 * BLOCK, B_GROUP * BLOCK), F32),
        name="band_bias",
    )(rel_bias, buckets)


def _swa_kernel(sink_ref, qt_ref, k_ref, vt_ref, bias_ref, o_ref, *, nblk):
    step = pl.program_id(1)
    per_step = SWA_TQ // BLOCK
    lane_head = lax.shift_right_logical(
        lax.broadcasted_iota(jnp.int32, (1, B_GROUP * BLOCK), 1), int(math.log2(BLOCK)))
    sinks = []
    for kvh in range(B_KV_HEADS):
        sv = jnp.zeros((1, B_GROUP * BLOCK), F32)
        for g in range(B_GROUP):
            sv = jnp.where(lane_head == g, sink_ref[kvh * B_GROUP + g] * LOG2E, sv)
        sinks.append(sv)

    def score(t, kvh):
        n = step * per_step + t
        nb0 = jnp.clip(n - 1, 0, nblk - 3)
        var = jnp.where(n == 0, 1, jnp.where(n == nblk - 1, 2, 0))
        kwin = k_ref[0, pl.ds(pl.multiple_of(nb0 * BLOCK, BLOCK), 3 * BLOCK), :]
        st = _dot(kwin, qt_ref[0, t, kvh]) + bias_ref[var, kvh]
        m = jnp.maximum(jnp.max(st, axis=0, keepdims=True), sinks[kvh])
        return st, m, nb0

    def attend(kvh, st, m, nb0):
        p = jnp.exp2(st - m).astype(BF16)
        acc = None
        for w in range(3):
            d = _dot(vt_ref[0, nb0 + w, kvh], p[w * BLOCK:(w + 1) * BLOCK])
            acc = d if acc is None else acc + d
        den = acc[B_HEAD_DIM:B_HEAD_DIM + 1] + jnp.exp2(sinks[kvh] - m)
        out = acc[:B_HEAD_DIM] / den
        return [out[:, g * BLOCK:(g + 1) * BLOCK] for g in range(B_GROUP)]

    def group(i, carry):
        chains = [(i * SWA_UNROLL + u, kvh) for u in range(SWA_UNROLL) for kvh in range(B_KV_HEADS)]
        scored = [score(*c) for c in chains[:SWA_LEAD]]
        rows = []
        for idx, (t, kvh) in enumerate(chains):
            if idx + SWA_LEAD < len(chains):
                scored.append(score(*chains[idx + SWA_LEAD]))
            rows += attend(kvh, *scored[idx])
            if kvh == B_KV_HEADS - 1:
                ot = jnp.concatenate(rows, axis=0)
                r0 = pl.multiple_of(t * BLOCK, BLOCK)
                o_ref[0, pl.ds(r0, BLOCK), :] = ot.T.astype(BF16)
                rows = []
        return carry

    lax.fori_loop(0, per_step // SWA_UNROLL, group, 0)


def _swa(sink, qst, ks, vst, biast):
    b, nblk = qst.shape[0], qst.shape[1]
    s = nblk * BLOCK
    per_step = SWA_TQ // BLOCK
    grid = (b, s // SWA_TQ)
    return pl.pallas_call(
        functools.partial(_swa_kernel, nblk=nblk),
        grid=grid,
        in_specs=[
            pl.BlockSpec(memory_space=pltpu.SMEM),
            pl.BlockSpec((1, per_step, B_KV_HEADS, LANES, B_GROUP * BLOCK),
                         lambda i, j: (i, j, 0, 0, 0)),
            pl.BlockSpec((1, s, LANES), lambda i, j: (i, 0, 0)),
            pl.BlockSpec((1, nblk, B_KV_HEADS, V_ROWS, BLOCK), lambda i, j: (i, 0, 0, 0, 0)),
            _const_spec((3, B_KV_HEADS, 3 * BLOCK, B_GROUP * BLOCK)),
        ],
        out_specs=pl.BlockSpec((1, SWA_TQ, SWA_COLS), lambda i, j: (i, j, 0)),
        out_shape=jax.ShapeDtypeStruct((b, s, SWA_COLS), BF16),
        compiler_params=pltpu.CompilerParams(
            dimension_semantics=("parallel", "arbitrary"), vmem_limit_bytes=VMEM_LIMIT),
        name="swa",
    )(sink, qst, ks, vst, biast)


def _t5_bucket(rel):
    nb = REL_BUCKETS // 2
    max_exact = nb // 2
    bucket = jnp.where(rel > 0, nb, 0)
    n = jnp.abs(rel)
    nf = jnp.maximum(n, 1).astype(jnp.float32)
    large = max_exact + (jnp.log(nf / max_exact) / math.log(REL_MAX_DIST / max_exact)
                         * (nb - max_exact)).astype(jnp.int32)
    large = jnp.minimum(large, nb - 1)
    return bucket + jnp.where(n < max_exact, n, large)


def _band_buckets():
    jj = jnp.arange(3 * BLOCK)[None, :, None]
    r = jnp.arange(BLOCK)[None, None, :]
    shift = jnp.array([0, BLOCK, -BLOCK])[:, None, None]
    j = jj + shift
    rel = j - BLOCK - r
    ok = (j >= 0) & (j < 3 * BLOCK) & (jnp.abs(rel) <= WINDOW)
    return jnp.where(ok, _t5_bucket(rel), -1).astype(jnp.int32)


def _rope_tables(seq):
    pos = jnp.arange(seq, dtype=jnp.float32)
    inv = ROPE_THETA ** (-jnp.arange(0, A_ROPE, 2, dtype=jnp.float32) / A_ROPE)
    ang = pos[:, None] * inv[None, :]
    cos, sin = jnp.cos(ang), jnp.sin(ang)
    cos2 = jnp.concatenate([cos, cos], axis=1)
    sin2 = jnp.concatenate([sin, sin], axis=1)
    z32 = jnp.zeros((seq, LANES - A_NOPE - A_ROPE), F32)
    ck = jnp.concatenate([jnp.zeros((seq, A_NOPE), F32), cos2, z32], axis=1)
    sk = jnp.concatenate([jnp.zeros((seq, A_NOPE), F32), sin2, z32], axis=1)
    scale = (A_NOPE + A_ROPE) ** -0.5 * LOG2E
    cq = jnp.concatenate([jnp.ones((seq, A_NOPE), F32), cos2, z32], axis=1) * scale
    return cq.T, (sk * scale).T, ck, sk


def _layout_tables():
    hr = A_ROPE // 2
    rot_src = np.concatenate([np.arange(hr, A_ROPE), np.arange(0, hr)])
    rot_sgn = np.concatenate([-np.ones(hr), np.ones(hr)])

    kr0 = A_Q_RANK + A_KV_RANK
    qb0 = kr0 + A_ROPE
    kb0 = qb0 + SWA_COLS
    vb0 = kb0 + B_KV_HEADS * B_HEAD_DIM

    src = np.zeros(N_IN, np.int32)
    sgn = np.zeros(N_IN, np.float32)
    src[_C_Q:_C_KR] = np.arange(kr0)
    sgn[_C_Q:_C_KR] = 1.0
    src[_C_KR + A_NOPE:_C_KR + A_NOPE + A_ROPE] = kr0 + np.arange(A_ROPE)
    sgn[_C_KR + A_NOPE:_C_KR + A_NOPE + A_ROPE] = 1.0
    src[_C_KRR + A_NOPE:_C_KRR + A_NOPE + A_ROPE] = kr0 + rot_src
    sgn[_C_KRR + A_NOPE:_C_KRR + A_NOPE + A_ROPE] = rot_sgn
    src[_C_KS:N_IN] = kb0 + np.arange(B_KV_HEADS * B_HEAD_DIM)
    sgn[_C_KS:N_IN] = 1.0

    t_src = np.zeros(N_INT, np.int32)
    t_sgn = np.zeros(N_INT, np.float32)
    t_src[:_R_VS] = qb0 + np.arange(SWA_COLS)
    t_sgn[:_R_VS] = 1.0
    for kvh in range(B_KV_HEADS):
        lo = _R_VS + kvh * V_ROWS
        t_src[lo:lo + B_HEAD_DIM] = vb0 + kvh * B_HEAD_DIM + np.arange(B_HEAD_DIM)
        t_sgn[lo:lo + B_HEAD_DIM] = 1.0

    qd = A_NOPE + A_ROPE
    assert qd + A_ROPE == LANES
    q_src = np.zeros(A_HEADS * LANES, np.int32)
    q_sgn = np.zeros(A_HEADS * LANES, np.float32)
    for hd in range(A_HEADS):
        lo = hd * LANES
        q_src[lo:lo + qd] = hd * qd + np.arange(qd)
        q_sgn[lo:lo + qd] = 1.0
        q_src[lo + qd:lo + LANES] = hd * qd + A_NOPE + rot_src
        q_sgn[lo + qd:lo + LANES] = rot_sgn

    kvd = A_NOPE + A_V
    k_src = np.zeros(A_HEADS * LANES, np.int32)
    k_sgn = np.zeros(A_HEADS * LANES, np.float32)
    v_src = np.zeros(A_HEADS * V_ROWS, np.int32)
    v_sgn = np.zeros(A_HEADS * V_ROWS, np.float32)
    for hd in range(A_HEADS):
        k_src[hd * LANES:hd * LANES + A_NOPE] = hd * kvd + np.arange(A_NOPE)
        k_sgn[hd * LANES:hd * LANES + A_NOPE] = 1.0
        v_src[hd * V_ROWS:hd * V_ROWS + A_V] = hd * kvd + A_NOPE + np.arange(A_V)
        v_sgn[hd * V_ROWS:hd * V_ROWS + A_V] = 1.0
    return (src, sgn), (t_src, t_sgn), (q_src, q_sgn), (k_src, k_sgn), (v_src, v_sgn)


def _gather_cols(w, src, sgn):
    parts, c, n = [], 0, len(src)
    while c < n:
        e = c + 1
        while e < n and sgn[e] == sgn[c] and (sgn[c] == 0 or src[e] == src[e - 1] + 1):
            e += 1
        if sgn[c] == 0:
            parts.append(jnp.zeros(w.shape[:-1] + (e - c,), w.dtype))
        else:
            piece = w[..., int(src[c]):int(src[c]) + e - c]
            parts.append(piece if sgn[c] > 0 else -piece)
        c = e
    return jnp.concatenate(parts, axis=-1).astype(BF16)


def kernel(x, rel_bias, ffn1_pre_g, ffn1_w_gate, ffn1_w_up, ffn1_w_down, ffn1_post_g, mix_pre_g, w_in, mla_q_norm_g, mla_w_uq, mla_kv_norm_g, mla_w_ukv, swa_sink, w_out, mix_post_g, ffn2_pre_g, ffn2_w_gate, ffn2_w_up, ffn2_w_down, ffn2_post_g):
    b, s, d = x.shape
    depth = w_in.shape[0]
    assert d == D_MODEL and s % FLASH_TQ == 0 and s % SWA_TQ == 0 and s >= 3 * BLOCK
    assert s % PROJ_TM == 0 and PROJ_TM % FLASH_TK == 0
    assert FLASH_STEPS % 2 == 0 and FLASH_STEPS % (PROJ_TM // FLASH_TK) == 0
    assert (b * s) % FFN_TM == 0

    in_l, int_l, q_l, k_l, v_l = _layout_tables()
    cq, sq, ck, sk = _rope_tables(s)
    biast = _band_bias(rel_bias, _band_buckets())

    rows = lambda g: g.reshape(depth, 1, -1)
    bf = lambda w: w.astype(BF16)
    ffn1 = (rows(ffn1_pre_g), bf(ffn1_w_gate), bf(ffn1_w_up), bf(ffn1_w_down), rows(ffn1_post_g))
    ffn2 = (rows(ffn2_pre_g), bf(ffn2_w_gate), bf(ffn2_w_up), bf(ffn2_w_down), rows(ffn2_post_g))
    proj = (rows(mix_pre_g), _gather_cols(w_in, *in_l), _gather_cols(w_in, *int_l).swapaxes(1, 2),
            rows(mla_q_norm_g), _gather_cols(mla_w_uq, *q_l).swapaxes(1, 2),
            rows(mla_kv_norm_g), _gather_cols(mla_w_ukv, *k_l),
            _gather_cols(mla_w_ukv, *v_l).swapaxes(1, 2))
    wo, mix_g = bf(w_out), rows(mix_post_g)

    x2 = x.reshape(b * s, d)
    for i in range(depth):
        x2 = _ffn(i, x2, *ffn1)
        qt, k, vt, qst, ks, vst = _proj(i, x2.reshape(b, s, d), *proj, cq, sq, ck, sk)
        oa = _flash(qt, k, vt)
        ob = _swa(swa_sink[i], qst, ks, vst, biast)
        x2 = _ffn(i, x2, *ffn2, mixer=(oa.reshape(b * s, -1), ob.reshape(b * s, -1), wo, mix_g))
    return x2.reshape(b, s, d)
```

```python
import functools
import math

import numpy as np
import jax
import jax.numpy as jnp
from jax import lax
from jax.experimental import pallas as pl
from jax.experimental.pallas import tpu as pltpu

D_MODEL = 1024
A_HEADS = 8
A_NOPE = 64
A_ROPE = 32
A_V = 64
A_Q_RANK = 384
A_KV_RANK = 256
B_HEADS = 8
B_KV_HEADS = 2
B_HEAD_DIM = 64
B_GROUP = B_HEADS // B_KV_HEADS
WINDOW = 128
BLOCK = 128
REL_BUCKETS = 32
REL_MAX_DIST = 128
D_FF = 2816
FFN_RES_WEIGHT = 0.5
ROPE_THETA = 10000.0
EPS = 1e-6
NEG_INF = -1e30
LOG2E = math.log2(math.e)

LANES = 128
MXU_N = 256
V_ROWS = 80
SWA_COLS = B_HEADS * B_HEAD_DIM
VMEM_LIMIT = 56 * 1024 * 1024

FFN_TM = 1024
FFN_SUB = 512
FFN_FC = 256
PROJ_TM = 512
FLASH_TQ = 4096
FLASH_TK = 256
FLASH_STEPS = 4
SWA_TQ = 1024
SWA_UNROLL = 8
SWA_LEAD = 2

_C_Q = 0
_C_KV = _C_Q + A_Q_RANK
_C_KR = _C_KV + A_KV_RANK
_C_KRR = _C_KR + LANES
_C_KS = _C_KRR + LANES
N_IN = _C_KS + B_KV_HEADS * B_HEAD_DIM
_R_VS = SWA_COLS
N_INT = _R_VS + B_KV_HEADS * V_ROWS

BF16 = jnp.bfloat16
F32 = jnp.float32


def _dot(a, b):
    return jnp.dot(a, b, preferred_element_type=F32)


def _dot_nt(a, b):
    return lax.dot_general(a, b, (((1,), (1,)), ((), ())), preferred_element_type=F32)


def _rms(x, g):
    return x * lax.rsqrt(jnp.mean(x * x, axis=-1, keepdims=True) + EPS) * g


def _const_spec(shape, layer=None):
    nd = len(shape)
    if layer is None:
        return pl.BlockSpec(shape, lambda *_: (0,) * nd, pipeline_mode=pl.Buffered(1))
    return pl.BlockSpec((None,) + tuple(shape), lambda *_: (layer,) + (0,) * nd,
                        pipeline_mode=pl.Buffered(1))


def _ffn_kernel(*refs, mixer):
    if mixer:
        oa_ref, ob_ref, wo_ref, mg_ref, x_ref, pre_ref, wg_ref, wu_ref, wd_ref, post_ref, o_ref = refs
    else:
        x_ref, pre_ref, wg_ref, wu_ref, wd_ref, post_ref, o_ref = refs
    post = FFN_RES_WEIGHT * post_ref[...]
    for r in range(0, FFN_TM, FFN_SUB):
        rs = slice(r, r + FFN_SUB)
        if mixer:
            half = A_HEADS * A_V
            o = _dot(oa_ref[rs, :], wo_ref[:half, :]) + _dot(ob_ref[rs, :], wo_ref[half:, :])
            x = x_ref[rs, :] + _rms(o, mg_ref[...])
        else:
            x = x_ref[rs, :]
        h = _rms(x, pre_ref[...]).astype(BF16)
        acc = None
        for c in range(0, D_FF, FFN_FC):
            g = _dot(h, wg_ref[:, c:c + FFN_FC])
            u = _dot(h, wu_ref[:, c:c + FFN_FC])
            a = (g * u / (1.0 + jnp.exp(-g))).astype(BF16)
            d = _dot(a, wd_ref[c:c + FFN_FC, :])
            acc = d if acc is None else acc + d
        o_ref[rs, :] = x + _rms(acc, post)


def _ffn(layer, x2, pre_g, wg, wu, wd, post_g, mixer=None):
    m = x2.shape[0]
    row = pl.BlockSpec((FFN_TM, D_MODEL), lambda i: (i, 0))
    in_specs = [row, _const_spec((1, D_MODEL), layer), _const_spec((D_MODEL, D_FF), layer),
                _const_spec((D_MODEL, D_FF), layer), _const_spec((D_FF, D_MODEL), layer),
                _const_spec((1, D_MODEL), layer)]
    args = (x2, pre_g, wg, wu, wd, post_g)
    if mixer is not None:
        half = pl.BlockSpec((FFN_TM, mixer[0].shape[1]), lambda i: (i, 0))
        in_specs = [half, half, _const_spec((D_MODEL, D_MODEL), layer),
                    _const_spec((1, D_MODEL), layer)] + in_specs
        args = tuple(mixer) + args
    return pl.pallas_call(
        functools.partial(_ffn_kernel, mixer=mixer is not None),
        grid=(m // FFN_TM,),
        in_specs=in_specs,
        out_specs=row,
        out_shape=jax.ShapeDtypeStruct((m, D_MODEL), F32),
        compiler_params=pltpu.CompilerParams(
            dimension_semantics=("parallel",), vmem_limit_bytes=VMEM_LIMIT),
        name="mix_ffn" if mixer is not None else "ffn",
    )(*args)


def _proj_kernel(x_ref, pre_ref, win_ref, wint_ref, qn_ref, wqt_ref, kvn_ref, wk_ref, wvt_ref,
                 cq_ref, sq_ref, ck_ref, sk_ref,
                 qt_ref, k_ref, vt_ref, qst_ref, ks_ref, vst_ref):
    h = _rms(x_ref[0], pre_ref[...]).astype(BF16)
    z = _dot(h, win_ref[...])
    zt = _dot_nt(wint_ref[...], h)
    cqn = _rms(z[:, _C_Q:_C_KV], qn_ref[...]).astype(BF16)
    ckvn = _rms(z[:, _C_KV:_C_KR], kvn_ref[...]).astype(BF16)

    qt2 = _dot_nt(wqt_ref[...], cqn)
    r0, r1 = A_NOPE, A_NOPE + A_ROPE
    cq = cq_ref[...]
    sq = sq_ref[...]
    pad = jnp.zeros((LANES - r1, x_ref.shape[1]), F32)
    for hd in range(A_HEADS):
        blk = qt2[hd * LANES:(hd + 1) * LANES]
        rope = blk[r0:r1] * cq[r0:r1] + blk[r1:] * sq[r0:r1]
        qt_ref[0, hd] = jnp.concatenate([blk[:r0] * cq[:r0], rope, pad], axis=0).astype(BF16)

    kr = z[:, _C_KR:_C_KRR] * ck_ref[...] + z[:, _C_KRR:_C_KS] * sk_ref[...]
    kn = _dot(ckvn, wk_ref[...])
    for hd in range(A_HEADS):
        lo = hd * LANES
        k_ref[0, hd] = (kn[:, lo:lo + LANES] + kr).astype(BF16)

    vt = _dot_nt(wvt_ref[...], ckvn)
    ones_row = (lax.broadcasted_iota(jnp.int32, (V_ROWS, 1), 0) == A_V).astype(F32)
    for hd in range(A_HEADS):
        lo = hd * V_ROWS
        vt_ref[0, hd, 0] = (vt[lo:lo + V_ROWS] + ones_row).astype(BF16)

    ks_ref[0] = z[:, _C_KS:N_IN].astype(BF16)
    qall = zt[:_R_VS] * (B_HEAD_DIM ** -0.5 * LOG2E)
    zero = jnp.zeros((B_HEAD_DIM, B_GROUP * BLOCK), F32)
    for blk in range(x_ref.shape[1] // BLOCK):
        cols = slice(blk * BLOCK, (blk + 1) * BLOCK)
        for kvh in range(B_KV_HEADS):
            piece = jnp.concatenate(
                [qall[(kvh * B_GROUP + g) * B_HEAD_DIM:(kvh * B_GROUP + g + 1) * B_HEAD_DIM, cols]
                 for g in range(B_GROUP)], axis=1)
            parts = [piece if i == kvh else zero for i in range(B_KV_HEADS)]
            qst_ref[0, blk, kvh] = jnp.concatenate(parts, axis=0).astype(BF16)
            lo = _R_VS + kvh * V_ROWS
            vst_ref[0, blk, kvh] = (zt[lo:lo + V_ROWS, cols] + ones_row).astype(BF16)


def _proj(layer, x3, pre_g, win, wint, qn_g, wqt, kvn_g, wk, wvt, cq, sq, ck, sk):
    b, s, _ = x3.shape
    tm = PROJ_TM
    nt = s // tm
    nb = tm // BLOCK
    grid = (b, nt)
    in_specs = [
        pl.BlockSpec((1, tm, D_MODEL), lambda i, j: (i, j, 0)),
        _const_spec((1, D_MODEL), layer), _const_spec((D_MODEL, N_IN), layer),
        _const_spec((N_INT, D_MODEL), layer),
        _const_spec((1, A_Q_RANK), layer), _const_spec((A_HEADS * LANES, A_Q_RANK), layer),
        _const_spec((1, A_KV_RANK), layer), _const_spec((A_KV_RANK, A_HEADS * LANES), layer),
        _const_spec((A_HEADS * V_ROWS, A_KV_RANK), layer),
        pl.BlockSpec((LANES, tm), lambda i, j: (0, j)),
        pl.BlockSpec((LANES, tm), lambda i, j: (0, j)),
        pl.BlockSpec((tm, LANES), lambda i, j: (j, 0)),
        pl.BlockSpec((tm, LANES), lambda i, j: (j, 0)),
    ]
    out_specs = [
        pl.BlockSpec((1, A_HEADS, LANES, tm), lambda i, j: (i, 0, 0, j)),
        pl.BlockSpec((1, A_HEADS, tm, LANES), lambda i, j: (i, 0, j, 0)),
        pl.BlockSpec((1, A_HEADS, 1, V_ROWS, tm), lambda i, j: (i, 0, j, 0, 0)),
        pl.BlockSpec((1, nb, B_KV_HEADS, LANES, B_GROUP * BLOCK), lambda i, j: (i, j, 0, 0, 0)),
        pl.BlockSpec((1, tm, LANES), lambda i, j: (i, j, 0)),
        pl.BlockSpec((1, nb, B_KV_HEADS, V_ROWS, BLOCK), lambda i, j: (i, j, 0, 0, 0)),
    ]
    out_shape = [
        jax.ShapeDtypeStruct((b, A_HEADS, LANES, s), BF16),
        jax.ShapeDtypeStruct((b, A_HEADS, s, LANES), BF16),
        jax.ShapeDtypeStruct((b, A_HEADS, nt, V_ROWS, tm), BF16),
        jax.ShapeDtypeStruct((b, s // BLOCK, B_KV_HEADS, LANES, B_GROUP * BLOCK), BF16),
        jax.ShapeDtypeStruct((b, s, LANES), BF16),
        jax.ShapeDtypeStruct((b, s // BLOCK, B_KV_HEADS, V_ROWS, BLOCK), BF16),
    ]
    return pl.pallas_call(
        _proj_kernel, grid=grid, in_specs=in_specs, out_specs=out_specs, out_shape=out_shape,
        compiler_params=pltpu.CompilerParams(
            dimension_semantics=("parallel", "parallel"), vmem_limit_bytes=VMEM_LIMIT),
        name="mix_proj",
    )(x3, pre_g, win, wint, qn_g, wqt, kvn_g, wk, wvt, cq, sq, ck, sk)


def _flash_kernel(qt_ref, k_ref, vt_ref, o_ref, sa_ref, sb_ref, mca_ref, mcb_ref, m_ref, acc_ref,
                  *, nk):
    tk = FLASH_TK
    per = PROJ_TM // tk
    bufs = ((sa_ref, mca_ref), (sb_ref, mcb_ref))
    m_ref[...] = jnp.full(m_ref.shape, NEG_INF, F32)
    acc_ref[...] = jnp.zeros(acc_ref.shape, F32)

    tiles = [(j, slice(n * MXU_N, (n + 1) * MXU_N))
             for j in range(2) for n in range(qt_ref.shape[3] // MXU_N)]

    def score_tile(c, buf, j, ln):
        s_ref, mc_ref = buf
        off = pl.multiple_of(c * tk, tk)
        st = _dot(k_ref[0, j, pl.ds(off, tk), :], qt_ref[0, j, :, ln])
        s_ref[j, :, ln] = st
        mc_ref[j, :, ln] = jnp.max(st, axis=0, keepdims=True)

    def update_tile(vidx, vpart, buf, j, ln):
        s_ref, mc_ref = buf
        m_prev = m_ref[j, :, ln]
        m_new = jnp.maximum(m_prev, mc_ref[j, :, ln])
        alpha = jnp.exp2(m_prev - m_new)
        p = jnp.exp2(s_ref[j, :, ln] - m_new).astype(BF16)
        vt = vt_ref[0, j, vidx, :, vpart * tk:(vpart + 1) * tk]
        acc_ref[j, :, ln] = alpha * acc_ref[j, :, ln] + _dot(vt, p)
        m_ref[j, :, ln] = m_new

    def step(c_next, buf_next, vidx, vpart, buf_cur):
        score_tile(c_next, buf_next, *tiles[0])
        for i, t in enumerate(tiles):
            if i + 1 < len(tiles):
                score_tile(c_next, buf_next, *tiles[i + 1])
            update_tile(vidx, vpart, buf_cur, *t)

    for t in tiles:
        score_tile(0, bufs[0], *t)

    def body(i, carry):
        for h in range(FLASH_STEPS):
            step(FLASH_STEPS * i + h + 1, bufs[(h + 1) % 2],
                 (FLASH_STEPS // per) * i + h // per, h % per, bufs[h % 2])
        return carry

    n_loop = (nk - 1) // FLASH_STEPS
    lax.fori_loop(0, n_loop, body, 0)
    for c in range(n_loop * FLASH_STEPS, nk - 1):
        step(c + 1, bufs[(c + 1) % 2], c // per, c % per, bufs[c % 2])
    for t in tiles:
        update_tile((nk - 1) // per, (nk - 1) % per, bufs[(nk - 1) % 2], *t)
    outs = []
    for j in range(2):
        acc = acc_ref[j]
        outs.append(acc[:A_V] / acc[A_V:A_V + 1])
    o_ref[0] = jnp.concatenate(outs, axis=0).T.astype(BF16)


def _flash(qt, k, vt):
    b, _, _, s = qt.shape
    tq = FLASH_TQ
    nk = s // FLASH_TK
    grid = (b, A_HEADS // 2, s // tq)
    return pl.pallas_call(
        functools.partial(_flash_kernel, nk=nk),
        grid=grid,
        in_specs=[
            pl.BlockSpec((1, 2, LANES, tq), lambda i, h, j: (i, h, 0, j)),
            pl.BlockSpec((1, 2, s, LANES), lambda i, h, j: (i, h, 0, 0)),
            pl.BlockSpec((1, 2, s // PROJ_TM, V_ROWS, PROJ_TM), lambda i, h, j: (i, h, 0, 0, 0)),
        ],
        out_specs=pl.BlockSpec((1, tq, LANES), lambda i, h, j: (i, j, h)),
        out_shape=jax.ShapeDtypeStruct((b, s, A_HEADS * A_V), BF16),
        scratch_shapes=[pltpu.VMEM((2, FLASH_TK, tq), F32), pltpu.VMEM((2, FLASH_TK, tq), F32),
                        pltpu.VMEM((2, 1, tq), F32), pltpu.VMEM((2, 1, tq), F32),
                        pltpu.VMEM((2, 1, tq), F32), pltpu.VMEM((2, V_ROWS, tq), F32)],
        compiler_params=pltpu.CompilerParams(
            dimension_semantics=("parallel", "parallel", "arbitrary"),
            vmem_limit_bytes=VMEM_LIMIT),
        name="mla_flash",
    )(qt, k, vt)


def _bias_kernel(tab_ref, bkt_ref, o_ref):
    for v in range(3):
        bkt = bkt_ref[v]
        for hd in range(B_HEADS):
            acc = jnp.full(bkt.shape, NEG_INF, F32)
            for i in range(REL_BUCKETS):
                acc = jnp.where(bkt == i, tab_ref[i, hd] * LOG2E, acc)
            kvh, g = divmod(hd, B_GROUP)
            o_ref[v, kvh, :, g * BLOCK:(g + 1) * BLOCK] = acc


def _band_bias(rel_bias, buckets):
    return pl.pallas_call(
        _bias_kernel,
        in_specs=[pl.BlockSpec(memory_space=pltpu.SMEM),
                  pl.BlockSpec(memory_space=pltpu.VMEM)],
        out_specs=pl.BlockSpec(memory_space=pltpu.VMEM),
        out_shape=jax.ShapeDtypeStruct((3, B_KV_HEADS, 3 * BLOCK, B_GROUP * BLOCK), F32),
        name="band_bias",
    )(rel_bias, buckets)


def _swa_kernel(sink_ref, qt_ref, k_ref, vt_ref, bias_ref, o_ref, *, nblk):
    step = pl.program_id(1)
    per_step = SWA_TQ // BLOCK
    lane_head = lax.shift_right_logical(
        lax.broadcasted_iota(jnp.int32, (1, B_GROUP * BLOCK), 1), int(math.log2(BLOCK)))
    sinks = []
    for kvh in range(B_KV_HEADS):
        sv = jnp.zeros((1, B_GROUP * BLOCK), F32)
        for g in range(B_GROUP):
            sv = jnp.where(lane_head == g, sink_ref[kvh * B_GROUP + g] * LOG2E, sv)
        sinks.append(sv)

    def score(t, kvh):
        n = step * per_step + t
        nb0 = jnp.clip(n - 1, 0, nblk - 3)
        var = jnp.where(n == 0, 1, jnp.where(n == nblk - 1, 2, 0))
        kwin = k_ref[0, pl.ds(pl.multiple_of(nb0 * BLOCK, BLOCK), 3 * BLOCK), :]
        st = _dot(kwin, qt_ref[0, t, kvh]) + bias_ref[var, kvh]
        m = jnp.maximum(jnp.max(st, axis=0, keepdims=True), sinks[kvh])
        return st, m, nb0

    def attend(kvh, st, m, nb0):
        p = jnp.exp2(st - m).astype(BF16)
        acc = None
        for w in range(3):
            d = _dot(vt_ref[0, nb0 + w, kvh], p[w * BLOCK:(w + 1) * BLOCK])
            acc = d if acc is None else acc + d
        den = acc[B_HEAD_DIM:B_HEAD_DIM + 1] + jnp.exp2(sinks[kvh] - m)
        out = acc[:B_HEAD_DIM] / den
        return [out[:, g * BLOCK:(g + 1) * BLOCK] for g in range(B_GROUP)]

    def group(i, carry):
        chains = [(i * SWA_UNROLL + u, kvh) for u in range(SWA_UNROLL) for kvh in range(B_KV_HEADS)]
        scored = [score(*c) for c in chains[:SWA_LEAD]]
        rows = []
        for idx, (t, kvh) in enumerate(chains):
            if idx + SWA_LEAD < len(chains):
                scored.append(score(*chains[idx + SWA_LEAD]))
            rows += attend(kvh, *scored[idx])
            if kvh == B_KV_HEADS - 1:
                ot = jnp.concatenate(rows, axis=0)
                r0 = pl.multiple_of(t * BLOCK, BLOCK)
                o_ref[0, pl.ds(r0, BLOCK), :] = ot.T.astype(BF16)
                rows = []
        return carry

    lax.fori_loop(0, per_step // SWA_UNROLL, group, 0)


def _swa(sink, qst, ks, vst, biast):
    b, nblk = qst.shape[0], qst.shape[1]
    s = nblk * BLOCK
    per_step = SWA_TQ // BLOCK
    grid = (b, s // SWA_TQ)
    return pl.pallas_call(
        functools.partial(_swa_kernel, nblk=nblk),
        grid=grid,
        in_specs=[
            pl.BlockSpec(memory_space=pltpu.SMEM),
            pl.BlockSpec((1, per_step, B_KV_HEADS, LANES, B_GROUP * BLOCK),
                         lambda i, j: (i, j, 0, 0, 0)),
            pl.BlockSpec((1, s, LANES), lambda i, j: (i, 0, 0)),
            pl.BlockSpec((1, nblk, B_KV_HEADS, V_ROWS, BLOCK), lambda i, j: (i, 0, 0, 0, 0)),
            _const_spec((3, B_KV_HEADS, 3 * BLOCK, B_GROUP * BLOCK)),
        ],
        out_specs=pl.BlockSpec((1, SWA_TQ, SWA_COLS), lambda i, j: (i, j, 0)),
        out_shape=jax.ShapeDtypeStruct((b, s, SWA_COLS), BF16),
        compiler_params=pltpu.CompilerParams(
            dimension_semantics=("parallel", "arbitrary"), vmem_limit_bytes=VMEM_LIMIT),
        name="swa",
    )(sink, qst, ks, vst, biast)


def _t5_bucket(rel):
    nb = REL_BUCKETS // 2
    max_exact = nb // 2
    bucket = jnp.where(rel > 0, nb, 0)
    n = jnp.abs(rel)
    nf = jnp.maximum(n, 1).astype(jnp.float32)
    large = max_exact + (jnp.log(nf / max_exact) / math.log(REL_MAX_DIST / max_exact)
                         * (nb - max_exact)).astype(jnp.int32)
    large = jnp.minimum(large, nb - 1)
    return bucket + jnp.where(n < max_exact, n, large)


def _band_buckets():
    jj = jnp.arange(3 * BLOCK)[None, :, None]
    r = jnp.arange(BLOCK)[None, None, :]
    shift = jnp.array([0, BLOCK, -BLOCK])[:, None, None]
    j = jj + shift
    rel = j - BLOCK - r
    ok = (j >= 0) & (j < 3 * BLOCK) & (jnp.abs(rel) <= WINDOW)
    return jnp.where(ok, _t5_bucket(rel), -1).astype(jnp.int32)


def _rope_tables(seq):
    pos = jnp.arange(seq, dtype=jnp.float32)
    inv = ROPE_THETA ** (-jnp.arange(0, A_ROPE, 2, dtype=jnp.float32) / A_ROPE)
    ang = pos[:, None] * inv[None, :]
    cos, sin = jnp.cos(ang), jnp.sin(ang)
    cos2 = jnp.concatenate([cos, cos], axis=1)
    sin2 = jnp.concatenate([sin, sin], axis=1)
    z32 = jnp.zeros((seq, LANES - A_NOPE - A_ROPE), F32)
    ck = jnp.concatenate([jnp.zeros((seq, A_NOPE), F32), cos2, z32], axis=1)
    sk = jnp.concatenate([jnp.zeros((seq, A_NOPE), F32), sin2, z32], axis=1)
    scale = (A_NOPE + A_ROPE) ** -0.5 * LOG2E
    cq = jnp.concatenate([jnp.ones((seq, A_NOPE), F32), cos2, z32], axis=1) * scale
    return cq.T, (sk * scale).T, ck, sk


def _layout_tables():
    hr = A_ROPE // 2
    rot_src = np.concatenate([np.arange(hr, A_ROPE), np.arange(0, hr)])
    rot_sgn = np.concatenate([-np.ones(hr), np.ones(hr)])

    kr0 = A_Q_RANK + A_KV_RANK
    qb0 = kr0 + A_ROPE
    kb0 = qb0 + SWA_COLS
    vb0 = kb0 + B_KV_HEADS * B_HEAD_DIM

    src = np.zeros(N_IN, np.int32)
    sgn = np.zeros(N_IN, np.float32)
    src[_C_Q:_C_KR] = np.arange(kr0)
    sgn[_C_Q:_C_KR] = 1.0
    src[_C_KR + A_NOPE:_C_KR + A_NOPE + A_ROPE] = kr0 + np.arange(A_ROPE)
    sgn[_C_KR + A_NOPE:_C_KR + A_NOPE + A_ROPE] = 1.0
    src[_C_KRR + A_NOPE:_C_KRR + A_NOPE + A_ROPE] = kr0 + rot_src
    sgn[_C_KRR + A_NOPE:_C_KRR + A_NOPE + A_ROPE] = rot_sgn
    src[_C_KS:N_IN] = kb0 + np.arange(B_KV_HEADS * B_HEAD_DIM)
    sgn[_C_KS:N_IN] = 1.0

    t_src = np.zeros(N_INT, np.int32)
    t_sgn = np.zeros(N_INT, np.float32)
    t_src[:_R_VS] = qb0 + np.arange(SWA_COLS)
    t_sgn[:_R_VS] = 1.0
    for kvh in range(B_KV_HEADS):
        lo = _R_VS + kvh * V_ROWS
        t_src[lo:lo + B_HEAD_DIM] = vb0 + kvh * B_HEAD_DIM + np.arange(B_HEAD_DIM)
        t_sgn[lo:lo + B_HEAD_DIM] = 1.0

    qd = A_NOPE + A_ROPE
    assert qd + A_ROPE == LANES
    q_src = np.zeros(A_HEADS * LANES, np.int32)
    q_sgn = np.zeros(A_HEADS * LANES, np.float32)
    for hd in range(A_HEADS):
        lo = hd * LANES
        q_src[lo:lo + qd] = hd * qd + np.arange(qd)
        q_sgn[lo:lo + qd] = 1.0
        q_src[lo + qd:lo + LANES] = hd * qd + A_NOPE + rot_src
        q_sgn[lo + qd:lo + LANES] = rot_sgn

    kvd = A_NOPE + A_V
    k_src = np.zeros(A_HEADS * LANES, np.int32)
    k_sgn = np.zeros(A_HEADS * LANES, np.float32)
    v_src = np.zeros(A_HEADS * V_ROWS, np.int32)
    v_sgn = np.zeros(A_HEADS * V_ROWS, np.float32)
    for hd in range(A_HEADS):
        k_src[hd * LANES:hd * LANES + A_NOPE] = hd * kvd + np.arange(A_NOPE)
        k_sgn[hd * LANES:hd * LANES + A_NOPE] = 1.0
        v_src[hd * V_ROWS:hd * V_ROWS + A_V] = hd * kvd + A_NOPE + np.arange(A_V)
        v_sgn[hd * V_ROWS:hd * V_ROWS + A_V] = 1.0
    return (src, sgn), (t_src, t_sgn), (q_src, q_sgn), (k_src, k_sgn), (v_src, v_sgn)


def _gather_cols(w, src, sgn):
    parts, c, n = [], 0, len(src)
    while c < n:
        e = c + 1
        while e < n and sgn[e] == sgn[c] and (sgn[c] == 0 or src[e] == src[e - 1] + 1):
            e += 1
        if sgn[c] == 0:
            parts.append(jnp.zeros(w.shape[:-1] + (e - c,), w.dtype))
        else:
            piece = w[..., int(src[c]):int(src[c]) + e - c]
            parts.append(piece if sgn[c] > 0 else -piece)
        c = e
    return jnp.concatenate(parts, axis=-1).astype(BF16)


def kernel(x, rel_bias, ffn1_pre_g, ffn1_w_gate, ffn1_w_up, ffn1_w_down, ffn1_post_g, mix_pre_g, w_in, mla_q_norm_g, mla_w_uq, mla_kv_norm_g, mla_w_ukv, swa_sink, w_out, mix_post_g, ffn2_pre_g, ffn2_w_gate, ffn2_w_up, ffn2_w_down, ffn2_post_g):
    b, s, d = x.shape
    depth = w_in.shape[0]
    assert d == D_MODEL and s % FLASH_TQ == 0 and s % SWA_TQ == 0 and s >= 3 * BLOCK
    assert s % PROJ_TM == 0 and PROJ_TM % FLASH_TK == 0
    assert FLASH_STEPS % 2 == 0 and FLASH_STEPS % (PROJ_TM // FLASH_TK) == 0
    assert (b * s) % FFN_TM == 0

    in_l, int_l, q_l, k_l, v_l = _layout_tables()
    cq, sq, ck, sk = _rope_tables(s)
    biast = _band_bias(rel_bias, _band_buckets())

    rows = lambda g: g.reshape(depth, 1, -1)
    bf = lambda w: w.astype(BF16)
    ffn1 = (rows(ffn1_pre_g), bf(ffn1_w_gate), bf(ffn1_w_up), bf(ffn1_w_down), rows(ffn1_post_g))
    ffn2 = (rows(ffn2_pre_g), bf(ffn2_w_gate), bf(ffn2_w_up), bf(ffn2_w_down), rows(ffn2_post_g))
    proj = (rows(mix_pre_g), _gather_cols(w_in, *in_l), _gather_cols(w_in, *int_l).swapaxes(1, 2),
            rows(mla_q_norm_g), _gather_cols(mla_w_uq, *q_l).swapaxes(1, 2),
            rows(mla_kv_norm_g), _gather_cols(mla_w_ukv, *k_l),
            _gather_cols(mla_w_ukv, *v_l).swapaxes(1, 2))
    wo, mix_g = bf(w_out), rows(mix_post_g)

    x2 = x.reshape(b * s, d)
    for i in range(depth):
        x2 = _ffn(i, x2, *ffn1)
        qt, k, vt, qst, ks, vst = _proj(i, x2.reshape(b, s, d), *proj, cq, sq, ck, sk)
        oa = _flash(qt, k, vt)
        ob = _swa(swa_sink[i], qst, ks, vst, biast)
        x2 = _ffn(i, x2, *ffn2, mixer=(oa.reshape(b * s, -1), ob.reshape(b * s, -1), wo, mix_g))
    return x2.reshape(b, s, d)
```

```python
import functools
import math

import numpy as np
import jax
import jax.numpy as jnp
from jax import lax
from jax.experimental import pallas as pl
from jax.experimental.pallas import tpu as pltpu

D_MODEL = 1024
A_HEADS = 8
A_NOPE = 64
A_ROPE = 32
A_V = 64
A_Q_RANK = 384
A_KV_RANK = 256
B_HEADS = 8
B_KV_HEADS = 2
B_HEAD_DIM = 64
B_GROUP = B_HEADS // B_KV_HEADS
WINDOW = 128
BLOCK = 128
REL_BUCKETS = 32
REL_MAX_DIST = 128
D_FF = 2816
FFN_RES_WEIGHT = 0.5
ROPE_THETA = 10000.0
EPS = 1e-6
NEG_INF = -1e30
LOG2E = math.log2(math.e)

LANES = 128
MXU_N = 256
V_ROWS = 80
SWA_COLS = B_HEADS * B_HEAD_DIM
VMEM_LIMIT = 56 * 1024 * 1024

FFN_TM = 1024
FFN_SUB = 512
FFN_FC = 256
PROJ_TM = 1024
PROJ_SUB = 512
FLASH_TQ = 4096
FLASH_TK = 256
FLASH_STEPS = 4
SWA_TQ = 1024
SWA_UNROLL = 8
SWA_LEAD = 2

_C_Q = 0
_C_KV = _C_Q + A_Q_RANK
_C_KR = _C_KV + A_KV_RANK
_C_KRR = _C_KR + LANES
_C_KS = _C_KRR + LANES
N_IN = _C_KS + B_KV_HEADS * B_HEAD_DIM
_R_VS = SWA_COLS
N_INT = _R_VS + B_KV_HEADS * V_ROWS

BF16 = jnp.bfloat16
F32 = jnp.float32


def _dot(a, b):
    return jnp.dot(a, b, preferred_element_type=F32)


def _dot_nt(a, b):
    return lax.dot_general(a, b, (((1,), (1,)), ((), ())), preferred_element_type=F32)


def _rms(x, g):
    return x * lax.rsqrt(jnp.mean(x * x, axis=-1, keepdims=True) + EPS) * g


def _const_spec(shape, layer=None):
    nd = len(shape)
    if layer is None:
        return pl.BlockSpec(shape, lambda *_: (0,) * nd, pipeline_mode=pl.Buffered(1))
    return pl.BlockSpec((None,) + tuple(shape), lambda *_: (layer,) + (0,) * nd,
                        pipeline_mode=pl.Buffered(1))


def _ffn_kernel(*refs, mixer):
    if mixer:
        oa_ref, ob_ref, wo_ref, mg_ref, x_ref, pre_ref, wg_ref, wu_ref, wd_ref, post_ref, o_ref = refs
    else:
        x_ref, pre_ref, wg_ref, wu_ref, wd_ref, post_ref, o_ref = refs
    post = FFN_RES_WEIGHT * post_ref[...]
    for r in range(0, FFN_TM, FFN_SUB):
        rs = slice(r, r + FFN_SUB)
        if mixer:
            half = A_HEADS * A_V
            o = _dot(oa_ref[rs, :], wo_ref[:half, :]) + _dot(ob_ref[rs, :], wo_ref[half:, :])
            x = x_ref[rs, :] + _rms(o, mg_ref[...])
        else:
            x = x_ref[rs, :]
        h = _rms(x, pre_ref[...]).astype(BF16)
        acc = None
        for c in range(0, D_FF, FFN_FC):
            g = _dot(h, wg_ref[:, c:c + FFN_FC])
            u = _dot(h, wu_ref[:, c:c + FFN_FC])
            a = (g * u / (1.0 + jnp.exp(-g))).astype(BF16)
            d = _dot(a, wd_ref[c:c + FFN_FC, :])
            acc = d if acc is None else acc + d
        o_ref[rs, :] = x + _rms(acc, post)


def _ffn(layer, x2, pre_g, wg, wu, wd, post_g, mixer=None):
    m = x2.shape[0]
    row = pl.BlockSpec((FFN_TM, D_MODEL), lambda i: (i, 0))
    in_specs = [row, _const_spec((1, D_MODEL), layer), _const_spec((D_MODEL, D_FF), layer),
                _const_spec((D_MODEL, D_FF), layer), _const_spec((D_FF, D_MODEL), layer),
                _const_spec((1, D_MODEL), layer)]
    args = (x2, pre_g, wg, wu, wd, post_g)
    if mixer is not None:
        half = pl.BlockSpec((FFN_TM, mixer[0].shape[1]), lambda i: (i, 0))
        in_specs = [half, half, _const_spec((D_MODEL, D_MODEL), layer),
                    _const_spec((1, D_MODEL), layer)] + in_specs
        args = tuple(mixer) + args
    return pl.pallas_call(
        functools.partial(_ffn_kernel, mixer=mixer is not None),
        grid=(m // FFN_TM,),
        in_specs=in_specs,
        out_specs=row,
        out_shape=jax.ShapeDtypeStruct((m, D_MODEL), F32),
        compiler_params=pltpu.CompilerParams(
            dimension_semantics=("parallel",), vmem_limit_bytes=VMEM_LIMIT),
        name="mix_ffn" if mixer is not None else "ffn",
    )(*args)


def _proj_kernel(x_ref, pre_ref, win_ref, wint_ref, qn_ref, wqt_ref, kvn_ref, wk_ref, wvt_ref,
                 cq_ref, sq_ref, ck_ref, sk_ref,
                 qt_ref, k_ref, vt_ref, qst_ref, ks_ref, vst_ref):
    ones_row = (lax.broadcasted_iota(jnp.int32, (V_ROWS, 1), 0) == A_V).astype(F32)
    r0, r1 = A_NOPE, A_NOPE + A_ROPE
    pad = jnp.zeros((LANES - r1, PROJ_SUB), F32)
    zero = jnp.zeros((B_HEAD_DIM, B_GROUP * BLOCK), F32)
    for si in range(PROJ_TM // PROJ_SUB):
        rs = slice(si * PROJ_SUB, (si + 1) * PROJ_SUB)
        h = _rms(x_ref[0, rs, :], pre_ref[...]).astype(BF16)
        z = _dot(h, win_ref[...])
        zt = _dot_nt(wint_ref[...], h)
        cqn = _rms(z[:, _C_Q:_C_KV], qn_ref[...]).astype(BF16)
        ckvn = _rms(z[:, _C_KV:_C_KR], kvn_ref[...]).astype(BF16)

        qt2 = _dot_nt(wqt_ref[...], cqn)
        cq = cq_ref[:, rs]
        sq = sq_ref[:, rs]
        for hd in range(A_HEADS):
            blk = qt2[hd * LANES:(hd + 1) * LANES]
            rope = blk[r0:r1] * cq[r0:r1] + blk[r1:] * sq[r0:r1]
            qt_ref[0, hd, :, rs] = jnp.concatenate(
                [blk[:r0] * cq[:r0], rope, pad], axis=0).astype(BF16)

        kr = z[:, _C_KR:_C_KRR] * ck_ref[rs, :] + z[:, _C_KRR:_C_KS] * sk_ref[rs, :]
        kn = _dot(ckvn, wk_ref[...])
        for hd in range(A_HEADS):
            lo = hd * LANES
            k_ref[0, hd, rs, :] = (kn[:, lo:lo + LANES] + kr).astype(BF16)

        vt = _dot_nt(wvt_ref[...], ckvn)
        for hd in range(A_HEADS):
            lo = hd * V_ROWS
            vt_ref[0, hd, si] = (vt[lo:lo + V_ROWS] + ones_row).astype(BF16)

        ks_ref[0, rs, :] = z[:, _C_KS:N_IN].astype(BF16)
        qall = zt[:_R_VS] * (B_HEAD_DIM ** -0.5 * LOG2E)
        for blk in range(PROJ_SUB // BLOCK):
            cols = slice(blk * BLOCK, (blk + 1) * BLOCK)
            ob = si * (PROJ_SUB // BLOCK) + blk
            for kvh in range(B_KV_HEADS):
                piece = jnp.concatenate(
                    [qall[(kvh * B_GROUP + g) * B_HEAD_DIM:(kvh * B_GROUP + g + 1) * B_HEAD_DIM, cols]
                     for g in range(B_GROUP)], axis=1)
                parts = [piece if i == kvh else zero for i in range(B_KV_HEADS)]
                qst_ref[0, ob, kvh] = jnp.concatenate(parts, axis=0).astype(BF16)
                lo = _R_VS + kvh * V_ROWS
                vst_ref[0, ob, kvh] = (zt[lo:lo + V_ROWS, cols] + ones_row).astype(BF16)


def _proj(layer, x3, pre_g, win, wint, qn_g, wqt, kvn_g, wk, wvt, cq, sq, ck, sk):
    b, s, _ = x3.shape
    tm = PROJ_TM
    nt = s // tm
    nb = tm // BLOCK
    grid = (b, nt)
    in_specs = [
        pl.BlockSpec((1, tm, D_MODEL), lambda i, j: (i, j, 0)),
        _const_spec((1, D_MODEL), layer), _const_spec((D_MODEL, N_IN), layer),
        _const_spec((N_INT, D_MODEL), layer),
        _const_spec((1, A_Q_RANK), layer), _const_spec((A_HEADS * LANES, A_Q_RANK), layer),
        _const_spec((1, A_KV_RANK), layer), _const_spec((A_KV_RANK, A_HEADS * LANES), layer),
        _const_spec((A_HEADS * V_ROWS, A_KV_RANK), layer),
        pl.BlockSpec((LANES, tm), lambda i, j: (0, j)),
        pl.BlockSpec((LANES, tm), lambda i, j: (0, j)),
        pl.BlockSpec((tm, LANES), lambda i, j: (j, 0)),
        pl.BlockSpec((tm, LANES), lambda i, j: (j, 0)),
    ]
    out_specs = [
        pl.BlockSpec((1, A_HEADS, LANES, tm), lambda i, j: (i, 0, 0, j)),
        pl.BlockSpec((1, A_HEADS, tm, LANES), lambda i, j: (i, 0, j, 0)),
        pl.BlockSpec((1, A_HEADS, tm // PROJ_SUB, V_ROWS, PROJ_SUB), lambda i, j: (i, 0, j, 0, 0)),
        pl.BlockSpec((1, nb, B_KV_HEADS, LANES, B_GROUP * BLOCK), lambda i, j: (i, j, 0, 0, 0)),
        pl.BlockSpec((1, tm, LANES), lambda i, j: (i, j, 0)),
        pl.BlockSpec((1, nb, B_KV_HEADS, V_ROWS, BLOCK), lambda i, j: (i, j, 0, 0, 0)),
    ]
    out_shape = [
        jax.ShapeDtypeStruct((b, A_HEADS, LANES, s), BF16),
        jax.ShapeDtypeStruct((b, A_HEADS, s, LANES), BF16),
        jax.ShapeDtypeStruct((b, A_HEADS, s // PROJ_SUB, V_ROWS, PROJ_SUB), BF16),
        jax.ShapeDtypeStruct((b, s // BLOCK, B_KV_HEADS, LANES, B_GROUP * BLOCK), BF16),
        jax.ShapeDtypeStruct((b, s, LANES), BF16),
        jax.ShapeDtypeStruct((b, s // BLOCK, B_KV_HEADS, V_ROWS, BLOCK), BF16),
    ]
    return pl.pallas_call(
        _proj_kernel, grid=grid, in_specs=in_specs, out_specs=out_specs, out_shape=out_shape,
        compiler_params=pltpu.CompilerParams(
            dimension_semantics=("parallel", "parallel"), vmem_limit_bytes=VMEM_LIMIT),
        name="mix_proj",
    )(x3, pre_g, win, wint, qn_g, wqt, kvn_g, wk, wvt, cq, sq, ck, sk)


def _flash_kernel(qt_ref, k_ref, vt_ref, o_ref, sa_ref, sb_ref, mca_ref, mcb_ref, m_ref, acc_ref,
                  *, nk):
    tk = FLASH_TK
    per = PROJ_SUB // tk
    bufs = ((sa_ref, mca_ref), (sb_ref, mcb_ref))
    m_ref[...] = jnp.full(m_ref.shape, NEG_INF, F32)
    acc_ref[...] = jnp.zeros(acc_ref.shape, F32)

    tiles = [(j, slice(n * MXU_N, (n + 1) * MXU_N))
             for j in range(2) for n in range(qt_ref.shape[3] // MXU_N)]

    def score_tile(c, buf, j, ln):
        s_ref, mc_ref = buf
        off = pl.multiple_of(c * tk, tk)
        st = _dot(k_ref[0, j, pl.ds(off, tk), :], qt_ref[0, j, :, ln])
        s_ref[j, :, ln] = st
        mc_ref[j, :, ln] = jnp.max(st, axis=0, keepdims=True)

    def update_tile(vidx, vpart, buf, j, ln):
        s_ref, mc_ref = buf
        m_prev = m_ref[j, :, ln]
        m_new = jnp.maximum(m_prev, mc_ref[j, :, ln])
        alpha = jnp.exp2(m_prev - m_new)
        p = jnp.exp2(s_ref[j, :, ln] - m_new).astype(BF16)
        vt = vt_ref[0, j, vidx, :, vpart * tk:(vpart + 1) * tk]
        acc_ref[j, :, ln] = alpha * acc_ref[j, :, ln] + _dot(vt, p)
        m_ref[j, :, ln] = m_new

    def step(c_next, buf_next, vidx, vpart, buf_cur):
        score_tile(c_next, buf_next, *tiles[0])
        for i, t in enumerate(tiles):
            if i + 1 < len(tiles):
                score_tile(c_next, buf_next, *tiles[i + 1])
            update_tile(vidx, vpart, buf_cur, *t)

    for t in tiles:
        score_tile(0, bufs[0], *t)

    def body(i, carry):
        for h in range(FLASH_STEPS):
            step(FLASH_STEPS * i + h + 1, bufs[(h + 1) % 2],
                 (FLASH_STEPS // per) * i + h // per, h % per, bufs[h % 2])
        return carry

    n_loop = (nk - 1) // FLASH_STEPS
    lax.fori_loop(0, n_loop, body, 0)
    for c in range(n_loop * FLASH_STEPS, nk - 1):
        step(c + 1, bufs[(c + 1) % 2], c // per, c % per, bufs[c % 2])
    for t in tiles:
        update_tile((nk - 1) // per, (nk - 1) % per, bufs[(nk - 1) % 2], *t)
    outs = []
    for j in range(2):
        acc = acc_ref[j]
        outs.append(acc[:A_V] / acc[A_V:A_V + 1])
    o_ref[0] = jnp.concatenate(outs, axis=0).T.astype(BF16)


def _flash(qt, k, vt):
    b, _, _, s = qt.shape
    tq = FLASH_TQ
    nk = s // FLASH_TK
    grid = (b, A_HEADS // 2, s // tq)
    return pl.pallas_call(
        functools.partial(_flash_kernel, nk=nk),
        grid=grid,
        in_specs=[
            pl.BlockSpec((1, 2, LANES, tq), lambda i, h, j: (i, h, 0, j)),
            pl.BlockSpec((1, 2, s, LANES), lambda i, h, j: (i, h, 0, 0)),
            pl.BlockSpec((1, 2, s // PROJ_SUB, V_ROWS, PROJ_SUB), lambda i, h, j: (i, h, 0, 0, 0)),
        ],
        out_specs=pl.BlockSpec((1, tq, LANES), lambda i, h, j: (i, j, h)),
        out_shape=jax.ShapeDtypeStruct((b, s, A_HEADS * A_V), BF16),
        scratch_shapes=[pltpu.VMEM((2, FLASH_TK, tq), F32), pltpu.VMEM((2, FLASH_TK, tq), F32),
                        pltpu.VMEM((2, 1, tq), F32), pltpu.VMEM((2, 1, tq), F32),
                        pltpu.VMEM((2, 1, tq), F32), pltpu.VMEM((2, V_ROWS, tq), F32)],
        compiler_params=pltpu.CompilerParams(
            dimension_semantics=("parallel", "parallel", "arbitrary"),
            vmem_limit_bytes=VMEM_LIMIT),
        name="mla_flash",
    )(qt, k, vt)


def _bias_kernel(tab_ref, bkt_ref, o_ref):
    for v in range(3):
        bkt = bkt_ref[v]
        for hd in range(B_HEADS):
            acc = jnp.full(bkt.shape, NEG_INF, F32)
            for i in range(REL_BUCKETS):
                acc = jnp.where(bkt == i, tab_ref[i, hd] * LOG2E, acc)
            kvh, g = divmod(hd, B_GROUP)
            o_ref[v, kvh, :, g * BLOCK:(g + 1) * BLOCK] = acc


def _band_bias(rel_bias, buckets):
    return pl.pallas_call(
        _bias_kernel,
        in_specs=[pl.BlockSpec(memory_space=pltpu.SMEM),
                  pl.BlockSpec(memory_space=pltpu.VMEM)],
        out_specs=pl.BlockSpec(memory_space=pltpu.VMEM),
        out_shape=jax.ShapeDtypeStruct((3, B_KV_HEADS, 3 * BLOCK, B_GROUP * BLOCK), F32),
        name="band_bias",
    )(rel_bias, buckets)


def _swa_kernel(sink_ref, qt_ref, k_ref, vt_ref, bias_ref, o_ref, *, nblk):
    step = pl.program_id(1)
    per_step = SWA_TQ // BLOCK
    lane_head = lax.shift_right_logical(
        lax.broadcasted_iota(jnp.int32, (1, B_GROUP * BLOCK), 1), int(math.log2(BLOCK)))
    sinks = []
    for kvh in range(B_KV_HEADS):
        sv = jnp.zeros((1, B_GROUP * BLOCK), F32)
        for g in range(B_GROUP):
            sv = jnp.where(lane_head == g, sink_ref[kvh * B_GROUP + g] * LOG2E, sv)
        sinks.append(sv)

    def score(t, kvh):
        n = step * per_step + t
        nb0 = jnp.clip(n - 1, 0, nblk - 3)
        var = jnp.where(n == 0, 1, jnp.where(n == nblk - 1, 2, 0))
        kwin = k_ref[0, pl.ds(pl.multiple_of(nb0 * BLOCK, BLOCK), 3 * BLOCK), :]
        st = _dot(kwin, qt_ref[0, t, kvh]) + bias_ref[var, kvh]
        m = jnp.maximum(jnp.max(st, axis=0, keepdims=True), sinks[kvh])
        return st, m, nb0

    def attend(kvh, st, m, nb0):
        p = jnp.exp2(st - m).astype(BF16)
        acc = None
        for w in range(3):
            d = _dot(vt_ref[0, nb0 + w, kvh], p[w * BLOCK:(w + 1) * BLOCK])
            acc = d if acc is None else acc + d
        den = acc[B_HEAD_DIM:B_HEAD_DIM + 1] + jnp.exp2(sinks[kvh] - m)
        out = acc[:B_HEAD_DIM] / den
        return [out[:, g * BLOCK:(g + 1) * BLOCK] for g in range(B_GROUP)]

    def group(i, carry):
        chains = [(i * SWA_UNROLL + u, kvh) for u in range(SWA_UNROLL) for kvh in range(B_KV_HEADS)]
        scored = [score(*c) for c in chains[:SWA_LEAD]]
        rows = []
        for idx, (t, kvh) in enumerate(chains):
            if idx + SWA_LEAD < len(chains):
                scored.append(score(*chains[idx + SWA_LEAD]))
            rows += attend(kvh, *scored[idx])
            if kvh == B_KV_HEADS - 1:
                ot = jnp.concatenate(rows, axis=0)
                r0 = pl.multiple_of(t * BLOCK, BLOCK)
                o_ref[0, pl.ds(r0, BLOCK), :] = ot.T.astype(BF16)
                rows = []
        return carry

    lax.fori_loop(0, per_step // SWA_UNROLL, group, 0)


def _swa(sink, qst, ks, vst, biast):
    b, nblk = qst.shape[0], qst.shape[1]
    s = nblk * BLOCK
    per_step = SWA_TQ // BLOCK
    grid = (b, s // SWA_TQ)
    return pl.pallas_call(
        functools.partial(_swa_kernel, nblk=nblk),
        grid=grid,
        in_specs=[
            pl.BlockSpec(memory_space=pltpu.SMEM),
            pl.BlockSpec((1, per_step, B_KV_HEADS, LANES, B_GROUP * BLOCK),
                         lambda i, j: (i, j, 0, 0, 0)),
            pl.BlockSpec((1, s, LANES), lambda i, j: (i, 0, 0)),
            pl.BlockSpec((1, nblk, B_KV_HEADS, V_ROWS, BLOCK), lambda i, j: (i, 0, 0, 0, 0)),
            _const_spec((3, B_KV_HEADS, 3 * BLOCK, B_GROUP * BLOCK)),
        ],
        out_specs=pl.BlockSpec((1, SWA_TQ, SWA_COLS), lambda i, j: (i, j, 0)),
        out_shape=jax.ShapeDtypeStruct((b, s, SWA_COLS), BF16),
        compiler_params=pltpu.CompilerParams(
            dimension_semantics=("parallel", "arbitrary"), vmem_limit_bytes=VMEM_LIMIT),
        name="swa",
    )(sink, qst, ks, vst, biast)


def _t5_bucket(rel):
    nb = REL_BUCKETS // 2
    max_exact = nb // 2
    bucket = jnp.where(rel > 0, nb, 0)
    n = jnp.abs(rel)
    nf = jnp.maximum(n, 1).astype(jnp.float32)
    large = max_exact + (jnp.log(nf / max_exact) / math.log(REL_MAX_DIST / max_exact)
                         * (nb - max_exact)).astype(jnp.int32)
    large = jnp.minimum(large, nb - 1)
    return bucket + jnp.where(n < max_exact, n, large)


def _band_buckets():
    jj = jnp.arange(3 * BLOCK)[None, :, None]
    r = jnp.arange(BLOCK)[None, None, :]
    shift = jnp.array([0, BLOCK, -BLOCK])[:, None, None]
    j = jj + shift
    rel = j - BLOCK - r
    ok = (j >= 0) & (j < 3 * BLOCK) & (jnp.abs(rel) <= WINDOW)
    return jnp.where(ok, _t5_bucket(rel), -1).astype(jnp.int32)


def _rope_tables(seq):
    pos = jnp.arange(seq, dtype=jnp.float32)
    inv = ROPE_THETA ** (-jnp.arange(0, A_ROPE, 2, dtype=jnp.float32) / A_ROPE)
    ang = pos[:, None] * inv[None, :]
    cos, sin = jnp.cos(ang), jnp.sin(ang)
    cos2 = jnp.concatenate([cos, cos], axis=1)
    sin2 = jnp.concatenate([sin, sin], axis=1)
    z32 = jnp.zeros((seq, LANES - A_NOPE - A_ROPE), F32)
    ck = jnp.concatenate([jnp.zeros((seq, A_NOPE), F32), cos2, z32], axis=1)
    sk = jnp.concatenate([jnp.zeros((seq, A_NOPE), F32), sin2, z32], axis=1)
    scale = (A_NOPE + A_ROPE) ** -0.5 * LOG2E
    cq = jnp.concatenate([jnp.ones((seq, A_NOPE), F32), cos2, z32], axis=1) * scale
    return cq.T, (sk * scale).T, ck, sk


def _layout_tables():
    hr = A_ROPE // 2
    rot_src = np.concatenate([np.arange(hr, A_ROPE), np.arange(0, hr)])
    rot_sgn = np.concatenate([-np.ones(hr), np.ones(hr)])

    kr0 = A_Q_RANK + A_KV_RANK
    qb0 = kr0 + A_ROPE
    kb0 = qb0 + SWA_COLS
    vb0 = kb0 + B_KV_HEADS * B_HEAD_DIM

    src = np.zeros(N_IN, np.int32)
    sgn = np.zeros(N_IN, np.float32)
    src[_C_Q:_C_KR] = np.arange(kr0)
    sgn[_C_Q:_C_KR] = 1.0
    src[_C_KR + A_NOPE:_C_KR + A_NOPE + A_ROPE] = kr0 + np.arange(A_ROPE)
    sgn[_C_KR + A_NOPE:_C_KR + A_NOPE + A_ROPE] = 1.0
    src[_C_KRR + A_NOPE:_C_KRR + A_NOPE + A_ROPE] = kr0 + rot_src
    sgn[_C_KRR + A_NOPE:_C_KRR + A_NOPE + A_ROPE] = rot_sgn
    src[_C_KS:N_IN] = kb0 + np.arange(B_KV_HEADS * B_HEAD_DIM)
    sgn[_C_KS:N_IN] = 1.0

    t_src = np.zeros(N_INT, np.int32)
    t_sgn = np.zeros(N_INT, np.float32)
    t_src[:_R_VS] = qb0 + np.arange(SWA_COLS)
    t_sgn[:_R_VS] = 1.0
    for kvh in range(B_KV_HEADS):
        lo = _R_VS + kvh * V_ROWS
        t_src[lo:lo + B_HEAD_DIM] = vb0 + kvh * B_HEAD_DIM + np.arange(B_HEAD_DIM)
        t_sgn[lo:lo + B_HEAD_DIM] = 1.0

    qd = A_NOPE + A_ROPE
    assert qd + A_ROPE == LANES
    q_src = np.zeros(A_HEADS * LANES, np.int32)
    q_sgn = np.zeros(A_HEADS * LANES, np.float32)
    for hd in range(A_HEADS):
        lo = hd * LANES
        q_src[lo:lo + qd] = hd * qd + np.arange(qd)
        q_sgn[lo:lo + qd] = 1.0
        q_src[lo + qd:lo + LANES] = hd * qd + A_NOPE + rot_src
        q_sgn[lo + qd:lo + LANES] = rot_sgn

    kvd = A_NOPE + A_V
    k_src = np.zeros(A_HEADS * LANES, np.int32)
    k_sgn = np.zeros(A_HEADS * LANES, np.float32)
    v_src = np.zeros(A_HEADS * V_ROWS, np.int32)
    v_sgn = np.zeros(A_HEADS * V_ROWS, np.float32)
    for hd in range(A_HEADS):
        k_src[hd * LANES:hd * LANES + A_NOPE] = hd * kvd + np.arange(A_NOPE)
        k_sgn[hd * LANES:hd * LANES + A_NOPE] = 1.0
        v_src[hd * V_ROWS:hd * V_ROWS + A_V] = hd * kvd + A_NOPE + np.arange(A_V)
        v_sgn[hd * V_ROWS:hd * V_ROWS + A_V] = 1.0
    return (src, sgn), (t_src, t_sgn), (q_src, q_sgn), (k_src, k_sgn), (v_src, v_sgn)


def _gather_cols(w, src, sgn):
    parts, c, n = [], 0, len(src)
    while c < n:
        e = c + 1
        while e < n and sgn[e] == sgn[c] and (sgn[c] == 0 or src[e] == src[e - 1] + 1):
            e += 1
        if sgn[c] == 0:
            parts.append(jnp.zeros(w.shape[:-1] + (e - c,), w.dtype))
        else:
            piece = w[..., int(src[c]):int(src[c]) + e - c]
            parts.append(piece if sgn[c] > 0 else -piece)
        c = e
    return jnp.concatenate(parts, axis=-1).astype(BF16)


def kernel(x, rel_bias, ffn1_pre_g, ffn1_w_gate, ffn1_w_up, ffn1_w_down, ffn1_post_g, mix_pre_g, w_in, mla_q_norm_g, mla_w_uq, mla_kv_norm_g, mla_w_ukv, swa_sink, w_out, mix_post_g, ffn2_pre_g, ffn2_w_gate, ffn2_w_up, ffn2_w_down, ffn2_post_g):
    b, s, d = x.shape
    depth = w_in.shape[0]
    assert d == D_MODEL and s % FLASH_TQ == 0 and s % SWA_TQ == 0 and s >= 3 * BLOCK
    assert s % PROJ_TM == 0 and PROJ_TM % PROJ_SUB == 0 and PROJ_SUB % FLASH_TK == 0
    assert FLASH_STEPS % 2 == 0 and FLASH_STEPS % (PROJ_SUB // FLASH_TK) == 0
    assert (b * s) % FFN_TM == 0

    in_l, int_l, q_l, k_l, v_l = _layout_tables()
    cq, sq, ck, sk = _rope_tables(s)
    biast = _band_bias(rel_bias, _band_buckets())

    rows = lambda g: g.reshape(depth, 1, -1)
    bf = lambda w: w.astype(BF16)
    ffn1 = (rows(ffn1_pre_g), bf(ffn1_w_gate), bf(ffn1_w_up), bf(ffn1_w_down), rows(ffn1_post_g))
    ffn2 = (rows(ffn2_pre_g), bf(ffn2_w_gate), bf(ffn2_w_up), bf(ffn2_w_down), rows(ffn2_post_g))
    proj = (rows(mix_pre_g), _gather_cols(w_in, *in_l), _gather_cols(w_in, *int_l).swapaxes(1, 2),
            rows(mla_q_norm_g), _gather_cols(mla_w_uq, *q_l).swapaxes(1, 2),
            rows(mla_kv_norm_g), _gather_cols(mla_w_ukv, *k_l),
            _gather_cols(mla_w_ukv, *v_l).swapaxes(1, 2))
    wo, mix_g = bf(w_out), rows(mix_post_g)

    x2 = x.reshape(b * s, d)
    for i in range(depth):
        x2 = _ffn(i, x2, *ffn1)
        qt, k, vt, qst, ks, vst = _proj(i, x2.reshape(b, s, d), *proj, cq, sq, ck, sk)
        oa = _flash(qt, k, vt)
        ob = _swa(swa_sink[i], qst, ks, vst, biast)
        x2 = _ffn(i, x2, *ffn2, mixer=(oa.reshape(b * s, -1), ob.reshape(b * s, -1), wo, mix_g))
    return x2.reshape(b, s, d)
```
